```python
import jax, jax.numpy as jnp
from jax import lax
import numpy as np

D_MODEL = 2048
BATCH = 4
SEQ = 2048
DEPTH = 1
DEC_BATCH = 16
DEC_SEQ = 16
PAST_LEN = 4096

CHUNK = 64
MIX_WIDTH = D_MODEL
HG_WIDTH = MIX_WIDTH // 2
HG_HEAD_DIM = 128
HG_HEADS = HG_WIDTH // HG_HEAD_DIM
SB_WIDTH = MIX_WIDTH - HG_WIDTH
SB_HEAD_DIM = 128
SB_HEADS = SB_WIDTH // SB_HEAD_DIM
SB_BLOCK = 128
IN_SPLITS = (HG_WIDTH, HG_WIDTH, HG_WIDTH, HG_WIDTH, SB_WIDTH, SB_WIDTH, SB_WIDTH)
IN_COLS = sum(IN_SPLITS)
SPLIT_POINTS = [int(c) for c in np.cumsum(IN_SPLITS)[:-1]]
N_GROUPS = 4
EXPERTS_PER_GROUP = 8
N_EXPERTS = N_GROUPS * EXPERTS_PER_GROUP
TOP_K = 2
EXPERT_FF = D_MODEL // 4
NORM_EPS = 1e-6

kernel_name = "hymba_style_hgrn2_stickbreaking_hmoe_step"


def rms_norm(x, w):
    xf = x.astype(jnp.float32)
    y = xf * lax.rsqrt(jnp.mean(xf * xf, axis=-1, keepdims=True) + NORM_EPS)
    return (y * w.astype(jnp.float32)).astype(x.dtype)


def hgrn_block(S, inp):
    q, k, v, logf = inp
    c = q.shape[2]
    B = jnp.cumsum(logf, axis=2)
    o_inter = jnp.einsum('bhck,bhkv->bhcv', q * jnp.exp(B), S)
    causal = jnp.tril(jnp.ones((c, c), dtype=bool))
    diff = B[:, :, :, None, :] - B[:, :, None, :, :]
    decay = jnp.exp(jnp.where(causal[:, :, None], diff, -jnp.inf))
    A = jnp.einsum('bhtk,bhsk,bhtsk->bhts', q, k, decay)
    o = o_inter + jnp.einsum('bhts,bhsv->bhtv', A, v)
    B_last = B[:, :, -1:, :]
    S_new = jnp.exp(B_last[:, :, 0, :])[..., None] * S + jnp.einsum(
        'bhsk,bhsv->bhkv', k * jnp.exp(B_last - B), v)
    return S_new, o


def hgrn_scan(S0, q, k, v, logf, block):
    b, h, t, _ = q.shape
    nc = t // block
    to_blocks = lambda a: a.reshape(b, h, nc, block, a.shape[-1]).transpose(2, 0, 1, 3, 4)
    S, o = lax.scan(hgrn_block, S0, (to_blocks(q), to_blocks(k), to_blocks(v), to_blocks(logf)))
    o = o.transpose(1, 2, 0, 3, 4).reshape(b, h, t, -1)
    return o, S


def sb_attend(q, k, v, q_pos, k_pos):
    z = jnp.einsum('bhqd,bhkd->bhqk', q.astype(jnp.float32), k.astype(jnp.float32)) * (SB_HEAD_DIM ** -0.5)
    mask = k_pos[None, :] < q_pos[:, None]
    log_1mb = jnp.where(mask, jax.nn.log_sigmoid(-z), 0.0)
    rest = lax.cumsum(log_1mb, axis=3, reverse=True) - log_1mb
    log_a = jnp.where(mask, jax.nn.log_sigmoid(z) + rest, -jnp.inf)
    return jnp.einsum('bhqk,bhkd->bhqd', jnp.exp(log_a), v.astype(jnp.float32))


def sb_attention(q, k, v, q_offset):
    b, h, nq, d = q.shape
    blk = min(SB_BLOCK, nq)
    nb = nq // blk
    q_blocks = q.reshape(b, h, nb, blk, d).transpose(2, 0, 1, 3, 4)
    q_pos = (q_offset + jnp.arange(nq, dtype=jnp.int32)).reshape(nb, blk)
    k_pos = jnp.arange(k.shape[2], dtype=jnp.int32)
    o = lax.map(lambda a: sb_attend(a[0], k, v, a[1], k_pos), (q_blocks, q_pos))
    return o.transpose(1, 2, 0, 3, 4).reshape(b, h, nq, d)


def hier_moe(x, rgw, rgb, rew, reb, wg, wu, wd):
    b, t, d = x.shape
    xt = x.reshape(-1, d)
    pg = jax.nn.softmax((xt @ rgw).astype(jnp.float32) + rgb.astype(jnp.float32), axis=-1)
    gidx = jnp.argmax(pg, axis=-1)
    gw = jnp.max(pg, axis=-1)
    elog = ((xt @ rew).astype(jnp.float32) + reb.astype(jnp.float32)).reshape(-1, N_GROUPS, EXPERTS_PER_GROUP)
    elog_sel = jnp.einsum('tg,tge->te', jax.nn.one_hot(gidx, N_GROUPS, dtype=jnp.float32), elog)
    pe = jax.nn.softmax(elog_sel, axis=-1)
    topv, topi = lax.top_k(pe, TOP_K)
    topv = topv / jnp.sum(topv, axis=-1, keepdims=True)
    w = gw[:, None] * topv
    eidx = gidx[:, None] * EXPERTS_PER_GROUP + topi
    combine = jnp.einsum('tk,tke->te', w, jax.nn.one_hot(eidx, N_EXPERTS, dtype=jnp.float32)).astype(x.dtype)
    y = jnp.zeros_like(xt)
    for e in range(N_EXPERTS):
        hid = jax.nn.silu(xt @ wg[e]) * (xt @ wu[e])
        y = y + (hid @ wd[e]) * combine[:, e:e + 1]
    return y.reshape(b, t, d)


def trunk_layer(x, l, past_k, past_v, hg_state, w_in, hg_lb_logits, hg_norm_w, w_out, norm1_w, norm2_w,
                rgw, rgb, rew, reb, wg, wu, wd):
    f32 = jnp.float32
    b, t, _ = x.shape
    xn = rms_norm(x, norm1_w[l])
    proj = jnp.einsum('btd,dc->btc', xn, w_in[l])
    hq, hf, hi, hgate, sq, sk, sv = jnp.split(proj, SPLIT_POINTS, axis=-1)

    lb = jnp.cumsum(jax.nn.softmax(hg_lb_logits.astype(f32), axis=0), axis=0)[l]
    forget = lb + (1.0 - lb) * jax.nn.sigmoid(hf.astype(f32))
    hg_heads = lambda a: a.reshape(b, t, HG_HEADS, HG_HEAD_DIM).transpose(0, 2, 1, 3)
    q_h = hg_heads(jax.nn.silu(hq.astype(f32)))
    k_h = hg_heads(1.0 - forget)
    v_h = hg_heads(hi.astype(f32))
    logf_h = hg_heads(jnp.log(forget))
    if hg_state is None:
        S0 = jnp.zeros((b, HG_HEADS, HG_HEAD_DIM, HG_HEAD_DIM), f32)
    else:
        S0 = hg_state[l].astype(f32)
    o_h, S_new = hgrn_scan(S0, q_h, k_h, v_h, logf_h, min(CHUNK, t))
    o_h = rms_norm(o_h.transpose(0, 2, 1, 3), hg_norm_w[l]) * jax.nn.silu(
        hgate.astype(f32).reshape(b, t, HG_HEADS, HG_HEAD_DIM))
    o_hg = o_h.reshape(b, t, HG_WIDTH)

    sb_heads = lambda a: a.reshape(b, t, SB_HEADS, SB_HEAD_DIM).transpose(0, 2, 1, 3)
    q_s, k_s, v_s = sb_heads(sq), sb_heads(sk), sb_heads(sv)
    if past_k is None:
        k_all, v_all, offset = k_s, v_s, 0
    else:
        k_all = jnp.concatenate([past_k[l].astype(k_s.dtype), k_s], axis=2)
        v_all = jnp.concatenate([past_v[l].astype(v_s.dtype), v_s], axis=2)
        offset = past_k.shape[3]
    o_sb = sb_attention(q_s, k_all, v_all, offset).transpose(0, 2, 1, 3).reshape(b, t, SB_WIDTH)

    mixed = jnp.concatenate([o_hg, o_sb], axis=-1).astype(x.dtype)
    x = x + mixed @ w_out[l]
    x = x + hier_moe(rms_norm(x, norm2_w[l]), rgw[l], rgb[l], rew[l], reb[l], wg[l], wu[l], wd[l])
    return x, k_s, v_s, S_new.astype(x.dtype)


def setup_inputs(seed: int = 0) -> dict:
    key = jax.random.key(seed)
    ks = jax.random.split(key, 19)
    nrm = lambda k, shape, s=1.0: jax.random.normal(k, shape, jnp.float32) * s
    return {
        "x_prompt": nrm(ks[0], (BATCH, SEQ, D_MODEL)),
        "x_sample": nrm(ks[1], (DEC_BATCH, DEC_SEQ, D_MODEL)),
        "cache_sb_k": nrm(ks[2], (DEPTH, DEC_BATCH, SB_HEADS, PAST_LEN, SB_HEAD_DIM)),
        "cache_sb_v": nrm(ks[3], (DEPTH, DEC_BATCH, SB_HEADS, PAST_LEN, SB_HEAD_DIM)),
        "state_hgrn": nrm(ks[4], (DEPTH, DEC_BATCH, HG_HEADS, HG_HEAD_DIM, HG_HEAD_DIM), 0.5),
        "w_in": nrm(ks[5], (DEPTH, D_MODEL, IN_COLS), D_MODEL ** -0.5),
        "hg_lb_logits": nrm(ks[6], (DEPTH + 1, HG_WIDTH), 0.5),
        "hg_norm_w": 1.0 + nrm(ks[7], (DEPTH, HG_HEAD_DIM), 0.02),
        "w_out": nrm(ks[8], (DEPTH, MIX_WIDTH, D_MODEL), MIX_WIDTH ** -0.5),
        "norm1_w": 1.0 + nrm(ks[9], (DEPTH, D_MODEL), 0.02),
        "norm2_w": 1.0 + nrm(ks[10], (DEPTH, D_MODEL), 0.02),
        "router_group_w": nrm(ks[11], (DEPTH, D_MODEL, N_GROUPS), D_MODEL ** -0.5),
        "router_group_b": nrm(ks[12], (DEPTH, N_GROUPS), 0.01),
        "router_expert_w": nrm(ks[13], (DEPTH, D_MODEL, N_EXPERTS), D_MODEL ** -0.5),
        "router_expert_b": nrm(ks[14], (DEPTH, N_EXPERTS), 0.01),
        "expert_w_gate": nrm(ks[15], (DEPTH, N_EXPERTS, D_MODEL, EXPERT_FF), D_MODEL ** -0.5),
        "expert_w_up": nrm(ks[16], (DEPTH, N_EXPERTS, D_MODEL, EXPERT_FF), D_MODEL ** -0.5),
        "expert_w_down": nrm(ks[17], (DEPTH, N_EXPERTS, EXPERT_FF, D_MODEL), EXPERT_FF ** -0.5),
        "final_norm_w": 1.0 + nrm(ks[18], (D_MODEL,), 0.02),
    }


def reference(x_prompt, x_sample, cache_sb_k, cache_sb_v, state_hgrn, w_in, hg_lb_logits, hg_norm_w,
              w_out, norm1_w, norm2_w, router_group_w, router_group_b, router_expert_w, router_expert_b,
              expert_w_gate, expert_w_up, expert_w_down, final_norm_w):
    weights = (w_in, hg_lb_logits, hg_norm_w, w_out, norm1_w, norm2_w, router_group_w, router_group_b,
               router_expert_w, router_expert_b, expert_w_gate, expert_w_up, expert_w_down)
    hp, hs = x_prompt, x_sample
    pk, pv, ps, sk, sv, ss = [], [], [], [], [], []
    for l in range(DEPTH):
        hp, k1, v1, s1 = trunk_layer(hp, l, None, None, None, *weights)
        hs, k2, v2, s2 = trunk_layer(hs, l, cache_sb_k, cache_sb_v, state_hgrn, *weights)
        pk.append(k1); pv.append(v1); ps.append(s1)
        sk.append(k2); sv.append(v2); ss.append(s2)
    y_prompt = rms_norm(hp, final_norm_w)
    y_sample = rms_norm(hs, final_norm_w)
    return (y_prompt, y_sample, jnp.stack(pk), jnp.stack(pv), jnp.stack(ps),
            jnp.stack(sk), jnp.stack(sv), jnp.stack(ss))
```

```python
import functools

import jax
import jax.numpy as jnp
import numpy as np
from jax import lax
from jax.experimental import pallas as pl
from jax.experimental.pallas import tpu as pltpu

F32 = jnp.float32
BF16 = jnp.bfloat16
NORM_EPS = 1e-6
LANES = 128
HEAD_DIM = 128
HGRN_CHUNK = 64
HGRN_SUB = 16
N_GROUPS = 4
EXPERTS_PER_GROUP = 8
N_EXPERTS = N_GROUPS * EXPERTS_PER_GROUP
VMEM_LIMIT = 48 * 1024 * 1024
NEG_BIG = -1e30

_NT = (((1,), (1,)), ((), ()))
_TN = (((0,), (0,)), ((), ()))


def _params(sem):
    return pltpu.CompilerParams(dimension_semantics=sem, vmem_limit_bytes=VMEM_LIMIT)


def _sigmoid(x):
    return 1.0 / (1.0 + jnp.exp(-x))


def _split3(x):
    h = x.astype(BF16)
    r = x - h.astype(F32)
    m = r.astype(BF16)
    l = (r - m.astype(F32)).astype(BF16)
    return h, m, l


def _inproj_kernel(x_ref, nw_ref, w_ref, o_ref, xn_ref):
    @pl.when(pl.program_id(1) == 0)
    def _():
        x = x_ref[...]
        ms = jnp.mean(x * x, axis=-1, keepdims=True)
        xn_ref[...] = ((x * lax.rsqrt(ms + NORM_EPS)) * nw_ref[...]).astype(BF16)

    o_ref[...] = jnp.dot(xn_ref[...], w_ref[...], preferred_element_type=F32)


def _inproj(x, nw, w, tm, tn):
    n, d = x.shape
    c = w.shape[1]
    return pl.pallas_call(
        _inproj_kernel,
        grid=(n // tm, c // tn),
        in_specs=[pl.BlockSpec((tm, d), lambda i, j: (i, 0)),
                  pl.BlockSpec((1, d), lambda i, j: (0, 0)),
                  pl.BlockSpec((d, tn), lambda i, j: (0, j))],
        out_specs=pl.BlockSpec((tm, tn), lambda i, j: (i, j)),
        out_shape=jax.ShapeDtypeStruct((n, c), F32),
        scratch_shapes=[pltpu.VMEM((tm, d), BF16)],
        compiler_params=_params(("arbitrary", "arbitrary")),
        name="inproj",
    )(x, nw, w)


def _hgrn_kernel(*refs, seq, chunk, sub, has_state):
    if has_state:
        (hq_ref, hf_ref, hi_ref, hg_ref, lbl_ref, nw_ref, tri_ref, wsel_ref, s0_ref,
         o_ref, sout_ref, st_ref, k_scr, b_scr) = refs
    else:
        (hq_ref, hf_ref, hi_ref, hg_ref, lbl_ref, nw_ref, tri_ref, wsel_ref,
         o_ref, sout_ref, st_ref, k_scr, b_scr) = refs
    n_sub = chunk // sub

    lg = lbl_ref[...]
    lg = jnp.exp(lg - jnp.max(lg, axis=0, keepdims=True))
    lb = lg[0:1, :] / jnp.sum(lg, axis=0, keepdims=True)

    if has_state:
        st_ref[...] = s0_ref[0, 0].T
    else:
        st_ref[...] = jnp.zeros_like(st_ref)

    row = lax.broadcasted_iota(jnp.int32, (chunk, chunk), 0)
    col = lax.broadcasted_iota(jnp.int32, (chunk, chunk), 1)
    blk_diff = row // sub - col // sub
    diag_mask = (blk_diff == 0) & (col <= row)
    tri = tri_ref[...]
    wsel = wsel_ref[...]
    nw = nw_ref[...]

    def chunk_body(ci, carry):
        r0 = pl.multiple_of(ci * chunk, chunk)
        hq = hq_ref[pl.ds(r0, chunk), :]
        hf = hf_ref[pl.ds(r0, chunk), :]
        v = hi_ref[pl.ds(r0, chunk), :]
        hg = hg_ref[pl.ds(r0, chunk), :]
        q = hq * _sigmoid(hq)
        f = lb + (1.0 - lb) * _sigmoid(hf)
        k = 1.0 - f
        logf = jnp.log(f)
        l1, l2, l3 = _split3(logf)
        bcum = (jnp.dot(tri, l1, preferred_element_type=F32)
                + jnp.dot(tri, l2, preferred_element_type=F32)
                + jnp.dot(tri, l3, preferred_element_type=F32))
        k_scr[...] = k
        b_scr[...] = bcum
        vb = v.astype(BF16)

        st = st_ref[...]
        qd = (q * jnp.exp(bcum)).astype(BF16)
        o = lax.dot_general(qd, st.astype(BF16), _NT, preferred_element_type=F32)
        b_last = b_scr[pl.ds(chunk - 1, 1), :]
        kd = (k * jnp.exp(b_last - bcum)).astype(BF16)
        st_ref[...] = st * jnp.exp(b_last) + lax.dot_general(vb, kd, _TN, preferred_element_type=F32)

        slabs = []
        for r in range(sub):
            parts = []
            for i in range(n_sub):
                k_row = k_scr[pl.ds(i * sub + r, 1), :]
                b_row = b_scr[pl.ds(i * sub + r, 1), :]
                qi = q[i * sub:(i + 1) * sub]
                bi = bcum[i * sub:(i + 1) * sub]
                parts.append(qi * k_row * jnp.exp(jnp.minimum(bi - b_row, 0.0)))
            slab = parts[0] if n_sub == 1 else jnp.concatenate(parts, axis=0)
            slabs.append(slab.astype(BF16))
        xcat = jnp.concatenate(slabs, axis=1)
        a = jnp.where(diag_mask, jnp.dot(xcat, wsel, preferred_element_type=F32), 0.0)

        if n_sub > 1:
            starts = [None] + [b_scr[pl.ds(i * sub - 1, 1), :] for i in range(1, n_sub)]
            ends = [b_scr[pl.ds(i * sub + sub - 1, 1), :] for i in range(n_sub)]
            qh, kh = [], []
            for i in range(n_sub):
                qi = q[i * sub:(i + 1) * sub]
                bi = bcum[i * sub:(i + 1) * sub]
                ki = k[i * sub:(i + 1) * sub]
                qh.append(qi if i == 0 else qi * jnp.exp(bi - starts[i]))
                kh.append(ki * jnp.exp(ends[i] - bi))
            khat = jnp.concatenate(kh, axis=0).astype(BF16)
            for d in range(1, n_sub):
                parts = []
                for i in range(n_sub):
                    if i < d:
                        parts.append(jnp.zeros((sub, HEAD_DIM), F32))
                    elif d == 1:
                        parts.append(qh[i])
                    else:
                        parts.append(qh[i] * jnp.exp(starts[i] - ends[i - d]))
                qd_ = jnp.concatenate(parts, axis=0).astype(BF16)
                m = lax.dot_general(qd_, khat, _NT, preferred_element_type=F32)
                a = a + jnp.where(blk_diff == d, m, 0.0)

        o = o + jnp.dot(a.astype(BF16), vb, preferred_element_type=F32)
        ms = jnp.mean(o * o, axis=-1, keepdims=True)
        on = (o * lax.rsqrt(ms + NORM_EPS)) * nw
        o_ref[pl.ds(r0, chunk), :] = (on * (hg * _sigmoid(hg))).astype(o_ref.dtype)
        return carry

    lax.fori_loop(0, seq // chunk, chunk_body, 0)
    sout_ref[0, 0] = st_ref[...].T


def _hgrn_consts(chunk, sub):
    t = np.arange(chunk)
    tri = (t[None, :] <= t[:, None]).astype(np.float32)
    rows = np.arange(sub * HEAD_DIM) // HEAD_DIM
    wsel = (rows[:, None] == (t[None, :] % sub)).astype(np.float32)
    return jnp.asarray(tri, BF16), jnp.asarray(wsel, BF16)


def _hgrn(proj, lb_logits, norm_w, state, batch, seq, heads, col0):
    chunk = min(HGRN_CHUNK, seq)
    sub = min(HGRN_SUB, chunk)
    tri, wsel = _hgrn_consts(chunk, sub)
    has_state = state is not None

    def col_spec(g):
        return pl.BlockSpec((seq, HEAD_DIM), lambda b, h: (b, col0 + g * heads + h))

    in_specs = [col_spec(0), col_spec(1), col_spec(2), col_spec(3),
                pl.BlockSpec((lb_logits.shape[0], HEAD_DIM), lambda b, h: (0, h)),
                pl.BlockSpec((1, HEAD_DIM), lambda b, h: (0, 0)),
                pl.BlockSpec(tri.shape, lambda b, h: (0, 0)),
                pl.BlockSpec(wsel.shape, lambda b, h: (0, 0))]
    args = [proj, proj, proj, proj, lb_logits, norm_w, tri, wsel]
    if has_state:
        in_specs.append(pl.BlockSpec((1, 1, HEAD_DIM, HEAD_DIM), lambda b, h: (b, h, 0, 0)))
        args.append(state)
    return pl.pallas_call(
        functools.partial(_hgrn_kernel, seq=seq, chunk=chunk, sub=sub, has_state=has_state),
        grid=(batch, heads),
        in_specs=in_specs,
        out_specs=[pl.BlockSpec((seq, HEAD_DIM), lambda b, h: (b, h)),
                   pl.BlockSpec((1, 1, HEAD_DIM, HEAD_DIM), lambda b, h: (b, h, 0, 0))],
        out_shape=[jax.ShapeDtypeStruct((batch * seq, heads * HEAD_DIM), BF16),
                   jax.ShapeDtypeStruct((batch, heads, HEAD_DIM, HEAD_DIM), F32)],
        scratch_shapes=[pltpu.VMEM((HEAD_DIM, HEAD_DIM), F32),
                        pltpu.VMEM((chunk, HEAD_DIM), F32),
                        pltpu.VMEM((chunk, HEAD_DIM), F32)],
        compiler_params=_params(("arbitrary", "arbitrary")),
        name="hgrn_state" if has_state else "hgrn",
    )(*args)


def _sb_block(q, k, v, uext, c, mask):
    tk = k.shape[0]
    z = lax.dot_general(q, k, _NT, preferred_element_type=F32)
    sp = jnp.maximum(-z, 0.0) + jnp.log(1.0 + jnp.exp(-jnp.abs(z)))
    l = -(z + sp)
    if mask is not None:
        l = jnp.where(mask, l, 0.0)
    lh = l.astype(BF16)
    ll = (l - lh.astype(F32)).astype(BF16)
    s = jnp.dot(lh, uext, preferred_element_type=F32) + jnp.dot(ll, uext, preferred_element_type=F32)
    c_t = c if tk == LANES else jnp.concatenate([c] * (tk // LANES), axis=1)
    p = jnp.exp(s[:, :tk] + c_t - sp)
    if mask is not None:
        p = jnp.where(mask, p, 0.0)
    pv = jnp.dot(p.astype(BF16), v, preferred_element_type=F32)
    return pv, c + s[:, tk:]


def _sb_uext(tk):
    s = np.arange(tk)
    u = (s[:, None] > s[None, :]).astype(np.float32)
    return jnp.asarray(np.concatenate([u, np.ones((tk, LANES), np.float32)], axis=1), BF16)


def _sb_prompt_kernel(q_ref, k_ref, v_ref, u_ref, o_ref, ko_ref, vo_ref, acc_ref, c_ref, *, tq, scale):
    i = pl.program_id(2)

    @pl.when(i == 0)
    def _():
        ko_ref[0, 0] = k_ref[...]
        vo_ref[0, 0] = v_ref[...]

    q = (q_ref[...] * scale).astype(BF16)
    uext = u_ref[...]
    row = lax.broadcasted_iota(jnp.int32, (tq, tq), 0)
    col = lax.broadcasted_iota(jnp.int32, (tq, tq), 1)
    r0 = pl.multiple_of(i * tq, tq)
    kd = k_ref[pl.ds(r0, tq), :].astype(BF16)
    vd = v_ref[pl.ds(r0, tq), :].astype(BF16)
    pv, c = _sb_block(q, kd, vd, uext, jnp.zeros((tq, LANES), F32), col < row)
    acc_ref[...] = pv
    c_ref[...] = c

    def body(jj, carry):
        j0 = pl.multiple_of((i - 1 - jj) * tq, tq)
        kb = k_ref[pl.ds(j0, tq), :].astype(BF16)
        vb = v_ref[pl.ds(j0, tq), :].astype(BF16)
        pv, c = _sb_block(q, kb, vb, uext, c_ref[...], None)
        acc_ref[...] += pv
        c_ref[...] = c
        return carry

    lax.fori_loop(0, i, body, 0)
    o_ref[...] = acc_ref[...].astype(o_ref.dtype)


def _sb_prompt(proj, batch, seq, heads, col0, tq):
    nq = seq // tq
    uext = _sb_uext(tq)
    kv_shape = jax.ShapeDtypeStruct((batch, heads, seq, HEAD_DIM), F32)
    return pl.pallas_call(
        functools.partial(_sb_prompt_kernel, tq=tq, scale=HEAD_DIM ** -0.5),
        grid=(batch, heads, nq),
        in_specs=[pl.BlockSpec((tq, HEAD_DIM), lambda b, h, i: (b * nq + i, col0 + h)),
                  pl.BlockSpec((seq, HEAD_DIM), lambda b, h, i: (b, col0 + heads + h)),
                  pl.BlockSpec((seq, HEAD_DIM), lambda b, h, i: (b, col0 + 2 * heads + h)),
                  pl.BlockSpec(uext.shape, lambda b, h, i: (0, 0))],
        out_specs=[pl.BlockSpec((tq, HEAD_DIM), lambda b, h, i: (b * nq + i, h)),
                   pl.BlockSpec((1, 1, seq, HEAD_DIM), lambda b, h, i: (b, h, 0, 0)),
                   pl.BlockSpec((1, 1, seq, HEAD_DIM), lambda b, h, i: (b, h, 0, 0))],
        out_shape=[jax.ShapeDtypeStruct((batch * seq, heads * HEAD_DIM), BF16), kv_shape, kv_shape],
        scratch_shapes=[pltpu.VMEM((tq, HEAD_DIM), F32), pltpu.VMEM((tq, LANES), F32)],
        compiler_params=_params(("arbitrary", "arbitrary", "arbitrary")),
        name="sb_prompt",
    )(proj, proj, proj, uext)


def _sb_decode_kernel(q_ref, k_ref, v_ref, pk_ref, pv_ref, un_ref, up_ref, o_ref, acc_ref, c_ref,
                      *, seq, past, tk, scale):
    q = (q_ref[...] * scale).astype(BF16)
    pad = jnp.zeros((LANES - seq, HEAD_DIM), F32)
    kn = jnp.concatenate([k_ref[...], pad], axis=0).astype(BF16)
    vn = jnp.concatenate([v_ref[...], pad], axis=0).astype(BF16)
    row = lax.broadcasted_iota(jnp.int32, (seq, LANES), 0)
    col = lax.broadcasted_iota(jnp.int32, (seq, LANES), 1)
    pv, c = _sb_block(q, kn, vn, un_ref[...], jnp.zeros((seq, LANES), F32), col < row)
    acc_ref[...] = pv
    c_ref[...] = c
    uext = up_ref[...]
    nblk = past // tk

    def body(jj, carry):
        j0 = pl.multiple_of((nblk - 1 - jj) * tk, tk)
        kb = pk_ref[0, 0, pl.ds(j0, tk), :].astype(BF16)
        vb = pv_ref[0, 0, pl.ds(j0, tk), :].astype(BF16)
        pv, c = _sb_block(q, kb, vb, uext, c_ref[...], None)
        acc_ref[...] += pv
        c_ref[...] = c
        return carry

    lax.fori_loop(0, nblk, body, 0)
    o_ref[...] = acc_ref[...].astype(o_ref.dtype)


def _sb_decode(proj, past_k, past_v, batch, seq, heads, col0, tk):
    past = past_k.shape[2]
    un, up = _sb_uext(LANES), _sb_uext(tk)
    return pl.pallas_call(
        functools.partial(_sb_decode_kernel, seq=seq, past=past, tk=tk, scale=HEAD_DIM ** -0.5),
        grid=(batch, heads),
        in_specs=[pl.BlockSpec((seq, HEAD_DIM), lambda b, h: (b, col0 + h)),
                  pl.BlockSpec((seq, HEAD_DIM), lambda b, h: (b, col0 + heads + h)),
                  pl.BlockSpec((seq, HEAD_DIM), lambda b, h: (b, col0 + 2 * heads + h)),
                  pl.BlockSpec((1, 1, past, HEAD_DIM), lambda b, h: (b, h, 0, 0)),
                  pl.BlockSpec((1, 1, past, HEAD_DIM), lambda b, h: (b, h, 0, 0)),
                  pl.BlockSpec(un.shape, lambda b, h: (0, 0)),
                  pl.BlockSpec(up.shape, lambda b, h: (0, 0))],
        out_specs=pl.BlockSpec((seq, HEAD_DIM), lambda b, h: (b, h)),
        out_shape=jax.ShapeDtypeStruct((batch * seq, heads * HEAD_DIM), BF16),
        scratch_shapes=[pltpu.VMEM((seq, HEAD_DIM), F32), pltpu.VMEM((seq, LANES), F32)],
        compiler_params=_params(("arbitrary", "arbitrary")),
        name="sb_decode",
    )(proj, proj, proj, past_k, past_v, un, up)


def _route(logits):
    lane = lax.broadcasted_iota(jnp.int32, logits.shape, 1).astype(F32)
    first = lambda hit: jnp.min(jnp.where(hit, lane, float(LANES)), axis=-1, keepdims=True)
    is_g = lane < N_GROUPS
    gl = jnp.where(is_g, logits, NEG_BIG)
    gmax = jnp.max(gl, axis=-1, keepdims=True)
    gidx = first(gl == gmax)
    gsum = jnp.sum(jnp.where(is_g, jnp.exp(gl - gmax), 0.0), axis=-1, keepdims=True)
    gw = 1.0 / gsum
    lo = N_GROUPS + EXPERTS_PER_GROUP * gidx
    in_g = (lane >= lo) & (lane < lo + EXPERTS_PER_GROUP)
    el = jnp.where(in_g, logits, NEG_BIG)
    m1 = jnp.max(el, axis=-1, keepdims=True)
    i1 = first(in_g & (el == m1))
    rest = in_g & (lane != i1)
    el2 = jnp.where(rest, logits, NEG_BIG)
    m2 = jnp.max(el2, axis=-1, keepdims=True)
    i2 = first(rest & (el2 == m2))
    t = jnp.exp(m2 - m1)
    w0 = gw / (1.0 + t)
    w1 = gw * t / (1.0 + t)
    e0 = i1 - N_GROUPS
    e1 = i2 - N_GROUPS
    return jnp.where(lane == 0, e0, jnp.where(lane == 1, e1, jnp.where(lane == 2, w0, jnp.where(lane == 3, w1, 0.0))))


def _outproj_kernel(x_ref, a_ref, b_ref, wa_ref, wb_ref, nw_ref, rw_ref, rb_ref, *rest):
    x2_ref, xn_ref, r_ref = rest[-3:]
    acc = jnp.dot(a_ref[...], wa_ref[...], preferred_element_type=F32)
    acc = acc + jnp.dot(b_ref[...], wb_ref[...], preferred_element_type=F32)
    x2 = x_ref[...] + acc
    x2_ref[...] = x2
    ms = jnp.mean(x2 * x2, axis=-1, keepdims=True)
    xn = (x2 * lax.rsqrt(ms + NORM_EPS)) * nw_ref[...]
    xn_ref[...] = xn
    xh = xn.astype(BF16)
    xl = (xn - xh.astype(F32)).astype(BF16)
    wh = rw_ref[0]
    logits = (jnp.dot(xh, wh, preferred_element_type=F32) + jnp.dot(xh, rw_ref[1], preferred_element_type=F32)
              + jnp.dot(xl, wh, preferred_element_type=F32)) + rb_ref[...]
    r_ref[...] = _route(logits)


def _outproj(x, oa, ob, wa, wb, nw, rw, rb, tm, n_total, row0, prev):
    n, d = x.shape
    blk0 = row0 // tm
    row = lambda i: (i, 0)
    orow = lambda i: (blk0 + i, 0)
    const = lambda i: (0, 0)
    in_specs = [pl.BlockSpec((tm, d), row),
                pl.BlockSpec((tm, oa.shape[1]), row),
                pl.BlockSpec((tm, ob.shape[1]), row),
                pl.BlockSpec(wa.shape, const),
                pl.BlockSpec(wb.shape, const),
                pl.BlockSpec((1, d), const),
                pl.BlockSpec(rw.shape, lambda i: (0, 0, 0)),
                pl.BlockSpec((1, LANES), const)]
    args = [x, oa, ob, wa, wb, nw, rw, rb]
    aliases = {}
    if prev is not None:
        in_specs += [pl.BlockSpec(memory_space=pl.ANY)] * 3
        aliases = {len(args) + t: t for t in range(3)}
        args += list(prev)
    return pl.pallas_call(
        _outproj_kernel,
        grid=(n // tm,),
        in_specs=in_specs,
        out_specs=[pl.BlockSpec((tm, d), orow), pl.BlockSpec((tm, d), orow), pl.BlockSpec((tm, LANES), orow)],
        out_shape=[jax.ShapeDtypeStruct((n_total, d), F32), jax.ShapeDtypeStruct((n_total, d), F32),
                   jax.ShapeDtypeStruct((n_total, LANES), F32)],
        input_output_aliases=aliases,
        compiler_params=_params(("arbitrary",)),
        name="outproj",
    )(*args)


def _row_copy(src_hbm, row, dst, r, sem):
    return pltpu.make_async_copy(src_hbm.at[pl.ds(row, 1), :], dst.at[pl.ds(r, 1), :], sem)


def _gather_kernel(nused_ref, idx_ref, nxt_ref, x_hbm, o_ref, buf, sem, *, tm):
    m = pl.program_id(0)
    nused = nused_ref[0]
    slot = m % 2

    def issue(tbl, s):
        def go(r, carry):
            _row_copy(x_hbm, tbl[0, 0, r], buf.at[s], r, sem.at[s]).start()
            return carry
        lax.fori_loop(0, tm, go, 0)

    @pl.when((m == 0) & (nused > 0))
    def _():
        issue(idx_ref, 0)

    @pl.when(m + 1 < nused)
    def _():
        issue(nxt_ref, 1 - slot)

    @pl.when(m < nused)
    def _():
        def wait(r, carry):
            _row_copy(x_hbm, 0, buf.at[slot], r, sem.at[slot]).wait()
            return carry
        lax.fori_loop(0, tm, wait, 0)
        o_ref[...] = buf[slot].astype(o_ref.dtype)


def _gather(nused, row_token, x, tm):
    tiles = row_token.shape[0]
    d = x.shape[1]
    smem = functools.partial(pl.BlockSpec, memory_space=pltpu.SMEM)
    return pl.pallas_call(
        functools.partial(_gather_kernel, tm=tm),
        grid_spec=pltpu.PrefetchScalarGridSpec(
            num_scalar_prefetch=1,
            grid=(tiles,),
            in_specs=[smem((1, 1, tm), lambda m, nu: (m, 0, 0)),
                      smem((1, 1, tm), lambda m, nu: (jnp.minimum(m + 1, tiles - 1), 0, 0)),
                      pl.BlockSpec(memory_space=pl.ANY)],
            out_specs=pl.BlockSpec((tm, d), lambda m, nu: (jnp.minimum(m, jnp.maximum(nu[0] - 1, 0)), 0)),
            scratch_shapes=[pltpu.VMEM((2, tm, d), F32), pltpu.SemaphoreType.DMA((2,))]),
        out_shape=jax.ShapeDtypeStruct((tiles * tm, d), BF16),
        compiler_params=_params(("arbitrary",)),
        name="gather",
    )(nused, row_token, row_token, x)


def _experts_kernel(te_ref, nused_ref, x_ref, w_ref, wg_ref, wu_ref, wd_ref, o_ref, g_scr, u_scr, d_scr):
    m = pl.program_id(0)

    @pl.when(m < nused_ref[0])
    def _():
        fresh = (m == 0) | (te_ref[m] != te_ref[jnp.maximum(m - 1, 0)])

        @pl.when(fresh)
        def _():
            g_scr[...] = wg_ref[0].astype(BF16)
            u_scr[...] = wu_ref[0].astype(BF16)
            d_scr[...] = wd_ref[0].astype(BF16)

        x = x_ref[...]
        g = jnp.dot(x, g_scr[...], preferred_element_type=F32)
        u = jnp.dot(x, u_scr[...], preferred_element_type=F32)
        hid = ((g * _sigmoid(g)) * u).astype(BF16)
        o_ref[...] = jnp.dot(hid, d_scr[...], preferred_element_type=F32) * w_ref[...]


def _experts(tile_expert, nused, xs, row_w, wg, wu, wd, tm):
    rows, d = xs.shape
    tiles = rows // tm
    ff = wg.shape[2]
    last = lambda m, te, nu: jnp.minimum(m, jnp.maximum(nu[0] - 1, 0))
    return pl.pallas_call(
        _experts_kernel,
        grid_spec=pltpu.PrefetchScalarGridSpec(
            num_scalar_prefetch=2,
            grid=(tiles,),
            in_specs=[pl.BlockSpec((tm, d), lambda m, te, nu: (last(m, te, nu), 0)),
                      pl.BlockSpec((tm, 1), lambda m, te, nu: (last(m, te, nu), 0)),
                      pl.BlockSpec((1, d, ff), lambda m, te, nu: (te[m], 0, 0)),
                      pl.BlockSpec((1, d, ff), lambda m, te, nu: (te[m], 0, 0)),
                      pl.BlockSpec((1, ff, d), lambda m, te, nu: (te[m], 0, 0))],
            out_specs=pl.BlockSpec((tm, d), lambda m, te, nu: (last(m, te, nu), 0)),
            scratch_shapes=[pltpu.VMEM((d, ff), BF16), pltpu.VMEM((d, ff), BF16), pltpu.VMEM((ff, d), BF16)]),
        out_shape=jax.ShapeDtypeStruct((rows, d), F32),
        compiler_params=_params(("arbitrary",)),
        name="experts",
    )(tile_expert, nused, xs, row_w, wg, wu, wd)


def _combine_kernel(pos_ref, nxt_ref, x_ref, fw_ref, ys_hbm, o_ref, buf, sem, *, tm, ntiles):
    m = pl.program_id(0)
    slot = m % 2

    def issue(tbl, s):
        def go(r, carry):
            _row_copy(ys_hbm, tbl[0, 0, 2 * r], buf.at[s, 0], r, sem.at[s]).start()
            _row_copy(ys_hbm, tbl[0, 0, 2 * r + 1], buf.at[s, 1], r, sem.at[s]).start()
            return carry
        lax.fori_loop(0, tm, go, 0)

    @pl.when(m == 0)
    def _():
        issue(pos_ref, 0)

    @pl.when(m + 1 < ntiles)
    def _():
        issue(nxt_ref, 1 - slot)

    def wait(r, carry):
        _row_copy(ys_hbm, 0, buf.at[slot, 0], r, sem.at[slot]).wait()
        _row_copy(ys_hbm, 0, buf.at[slot, 1], r, sem.at[slot]).wait()
        return carry
    lax.fori_loop(0, tm, wait, 0)
    x = x_ref[...] + (buf[slot, 0] + buf[slot, 1])
    ms = jnp.mean(x * x, axis=-1, keepdims=True)
    o_ref[...] = (x * lax.rsqrt(ms + NORM_EPS)) * fw_ref[...]


def _combine(pos, x2, fw, ys, tm, row0, n):
    d = x2.shape[1]
    blk0 = row0 // tm
    ntiles = n // tm
    smem = functools.partial(pl.BlockSpec, memory_space=pltpu.SMEM)
    return pl.pallas_call(
        functools.partial(_combine_kernel, tm=tm, ntiles=ntiles),
        grid=(ntiles,),
        in_specs=[smem((1, 1, 2 * tm), lambda m: (blk0 + m, 0, 0)),
                  smem((1, 1, 2 * tm), lambda m: (blk0 + jnp.minimum(m + 1, ntiles - 1), 0, 0)),
                  pl.BlockSpec((tm, d), lambda m: (blk0 + m, 0)),
                  pl.BlockSpec((1, d), lambda m: (0, 0)),
                  pl.BlockSpec(memory_space=pl.ANY)],
        out_specs=pl.BlockSpec((tm, d), lambda m: (m, 0)),
        out_shape=jax.ShapeDtypeStruct((n, d), F32),
        scratch_shapes=[pltpu.VMEM((2, 2, tm, d), F32), pltpu.SemaphoreType.DMA((2,))],
        compiler_params=_params(("arbitrary",)),
        name="combine",
    )(pos, pos, x2, fw, ys)


def _dispatch_tables(r, tm, tiles):
    n = r.shape[0]
    e = r[:, 0:2].astype(jnp.int32).reshape(-1)
    w = r[:, 2:4].reshape(-1)
    onehot = (e[:, None] == jnp.arange(N_EXPERTS, dtype=jnp.int32)[None, :]).astype(jnp.int32)
    incl = jnp.cumsum(onehot, axis=0)
    rank = jnp.sum(onehot * incl, axis=1) - 1
    counts = incl[-1]
    etiles = (counts + tm - 1) // tm
    tile_end = jnp.cumsum(etiles)
    nused = tile_end[-1]
    pos = (tile_end - etiles)[e] * tm + rank
    tile_expert = jnp.searchsorted(tile_end, jnp.arange(tiles, dtype=jnp.int32), side="right")
    last_e = jnp.max(jnp.where(counts > 0, jnp.arange(N_EXPERTS), 0))
    tile_expert = jnp.minimum(tile_expert, last_e).astype(jnp.int32)
    token = jnp.arange(2 * n, dtype=jnp.int32) // 2
    row_token = jnp.zeros((tiles * tm,), jnp.int32).at[pos].set(token)
    row_w = jnp.zeros((tiles * tm,), F32).at[pos].set(w)
    return (tile_expert, nused.astype(jnp.int32).reshape(1), row_token.reshape(tiles, 1, tm),
            row_w.reshape(tiles * tm, 1), pos.astype(jnp.int32))


def kernel(x_prompt, x_sample, cache_sb_k, cache_sb_v, state_hgrn, w_in, hg_lb_logits, hg_norm_w, w_out, norm1_w,
           norm2_w, router_group_w, router_group_b, router_expert_w, router_expert_b, expert_w_gate, expert_w_up,
           expert_w_down, final_norm_w):
    bp, tp, d = x_prompt.shape
    bs, ts, _ = x_sample.shape
    np_, ns = bp * tp, bs * ts
    n = np_ + ns
    heads = w_in.shape[2] // (7 * HEAD_DIM)
    hw = heads * HEAD_DIM
    tm_tok = 256
    tm_moe = 256

    w_in_b = w_in[0].astype(BF16)
    w_out_b = w_out[0].astype(BF16)
    n1 = norm1_w[0].reshape(1, d)
    n2 = norm2_w[0].reshape(1, d)
    hnw = hg_norm_w[0].reshape(1, HEAD_DIM)
    fw = final_norm_w.reshape(1, d)

    xp = x_prompt.reshape(np_, d)
    xs = x_sample.reshape(ns, d)
    proj_p = _inproj(xp, n1, w_in_b, 1024, 512)
    proj_s = _inproj(xs, n1, w_in_b, ns, 512)

    ohg_p, st_p = _hgrn(proj_p, hg_lb_logits, hnw, None, bp, tp, heads, 0)
    ohg_s, st_s = _hgrn(proj_s, hg_lb_logits, hnw, state_hgrn[0], bs, ts, heads, 0)
    osb_p, k_p, v_p = _sb_prompt(proj_p, bp, tp, heads, 4 * heads, 256)
    osb_s = _sb_decode(proj_s, cache_sb_k[0], cache_sb_v[0], bs, ts, heads, 4 * heads, 512)

    rw = jnp.concatenate([router_group_w[0], router_expert_w[0]], axis=1)
    rw = jnp.pad(rw, ((0, 0), (0, LANES - rw.shape[1])))
    rwh = rw.astype(BF16)
    rw2 = jnp.stack([rwh, (rw - rwh.astype(F32)).astype(BF16)])
    rb = jnp.concatenate([router_group_b[0], router_expert_b[0]])
    rb = jnp.pad(rb, (0, LANES - rb.shape[0])).reshape(1, LANES)

    wa, wb = w_out_b[:hw], w_out_b[hw:]
    outs = _outproj(xp, ohg_p, osb_p, wa, wb, n2, rw2, rb, tm_tok, n, 0, None)
    x2, xn2, r = _outproj(xs, ohg_s, osb_s, wa, wb, n2, rw2, rb, tm_tok, n, np_, outs)

    tiles = (2 * n) // tm_moe + N_EXPERTS
    tile_expert, nused, row_token, row_w, pos = _dispatch_tables(r, tm_moe, tiles)
    xg = _gather(nused, row_token, xn2, tm_moe)
    ys = _experts(tile_expert, nused, xg, row_w, expert_w_gate[0], expert_w_up[0], expert_w_down[0], tm_moe)
    pos = pos.reshape(n // tm_tok, 1, 2 * tm_tok)
    y_p = _combine(pos, x2, fw, ys, tm_tok, 0, np_)
    y_s = _combine(pos, x2, fw, ys, tm_tok, np_, ns)

    heads_s = lambda a: a.reshape(bs, ts, heads, HEAD_DIM).transpose(0, 2, 1, 3)[None]
    k_s = heads_s(proj_s[:, 5 * hw:6 * hw])
    v_s = heads_s(proj_s[:, 6 * hw:7 * hw])
    return (y_p.reshape(bp, tp, d), y_s.reshape(bs, ts, d), k_p[None], v_p[None], st_p[None],
            k_s, v_s, st_s[None])
```

```python
import functools

import jax
import jax.numpy as jnp
import numpy as np
from jax import lax
from jax.experimental import pallas as pl
from jax.experimental.pallas import tpu as pltpu

F32 = jnp.float32
BF16 = jnp.bfloat16
NORM_EPS = 1e-6
LANES = 128
HEAD_DIM = 128
HGRN_CHUNK = 64
HGRN_SUB = 16
N_GROUPS = 4
EXPERTS_PER_GROUP = 8
N_EXPERTS = N_GROUPS * EXPERTS_PER_GROUP
VMEM_LIMIT = 48 * 1024 * 1024
NEG_BIG = -1e30
LOG2E = 1.4426950408889634

_NT = (((1,), (1,)), ((), ()))
_TN = (((0,), (0,)), ((), ()))


def _params(sem):
    return pltpu.CompilerParams(dimension_semantics=sem, vmem_limit_bytes=VMEM_LIMIT)


def _sigmoid(x):
    return 1.0 / (1.0 + jnp.exp(-x))


def _split3(x):
    h = x.astype(BF16)
    r = x - h.astype(F32)
    m = r.astype(BF16)
    l = (r - m.astype(F32)).astype(BF16)
    return h, m, l


def _cat(parts, axis):
    return parts[0] if len(parts) == 1 else jnp.concatenate(parts, axis=axis)


def _inproj_kernel(x_ref, nw_ref, w_ref, o_ref, xn_ref):
    @pl.when(pl.program_id(1) == 0)
    def _():
        x = x_ref[...]
        ms = jnp.mean(x * x, axis=-1, keepdims=True)
        xn_ref[...] = ((x * lax.rsqrt(ms + NORM_EPS)) * nw_ref[...]).astype(BF16)

    o_ref[...] = jnp.dot(xn_ref[...], w_ref[...], preferred_element_type=F32)


def _inproj(x, nw, w, tm, tn):
    n, d = x.shape
    c = w.shape[1]
    return pl.pallas_call(
        _inproj_kernel,
        grid=(n // tm, c // tn),
        in_specs=[pl.BlockSpec((tm, d), lambda i, j: (i, 0)),
                  pl.BlockSpec((1, d), lambda i, j: (0, 0)),
                  pl.BlockSpec((d, tn), lambda i, j: (0, j))],
        out_specs=pl.BlockSpec((tm, tn), lambda i, j: (i, j)),
        out_shape=jax.ShapeDtypeStruct((n, c), F32),
        scratch_shapes=[pltpu.VMEM((tm, d), BF16)],
        compiler_params=_params(("arbitrary", "arbitrary")),
        name="inproj",
    )(x, nw, w)


def _hgrn_kernel(*refs, tb, chunk, sub, hp, has_state):
    if has_state:
        (hq_ref, hf_ref, hi_ref, hg_ref, lbl_ref, nw_ref, tri_ref, wsel_ref, s0_ref,
         o_ref, sout_ref, st_ref, b_scr, c_scr, lk_scr) = refs
    else:
        (hq_ref, hf_ref, hi_ref, hg_ref, lbl_ref, nw_ref, tri_ref, wsel_ref,
         o_ref, sout_ref, st_ref, b_scr, c_scr, lk_scr) = refs
    n_sub = chunk // sub
    ti = pl.program_id(2)
    hs = [slice(h * HEAD_DIM, (h + 1) * HEAD_DIM) for h in range(hp)]

    lg = lbl_ref[...]
    lg = jnp.exp(lg - jnp.max(lg, axis=0, keepdims=True))
    lb = lg[0:1, :] / jnp.sum(lg, axis=0, keepdims=True)

    @pl.when(ti == 0)
    def _():
        for h in range(hp):
            st_ref[h] = s0_ref[0, h].T if has_state else jnp.zeros((HEAD_DIM, HEAD_DIM), F32)

    row = lax.broadcasted_iota(jnp.int32, (chunk, chunk), 0)
    col = lax.broadcasted_iota(jnp.int32, (chunk, chunk), 1)
    blk_diff = row // sub - col // sub
    diag_mask = (blk_diff == 0) & (col <= row)
    diag_mask_all = _cat([diag_mask] * hp, 0)
    tri = tri_ref[...]
    wsel = wsel_ref[...]
    nw = nw_ref[...]

    def chunk_body(ci, carry):
        r0 = pl.multiple_of(ci * chunk, chunk)
        hq = hq_ref[pl.ds(r0, chunk), :]
        hf = hf_ref[pl.ds(r0, chunk), :]
        v = hi_ref[pl.ds(r0, chunk), :]
        hg = hg_ref[pl.ds(r0, chunk), :]
        q = hq * _sigmoid(hq)
        f = lb + (1.0 - lb) * _sigmoid(hf)
        k = 1.0 - f
        lf = jnp.log(f) * LOG2E
        lk = jnp.log(k) * LOG2E
        l1, l2, l3 = _split3(lf)
        b = (jnp.dot(tri, l1, preferred_element_type=F32) + jnp.dot(tri, l2, preferred_element_type=F32)
             + jnp.dot(tri, l3, preferred_element_type=F32))
        c = b - lk
        b_scr[...] = b
        c_scr[...] = c
        lk_scr[...] = lk
        vb = v.astype(BF16)
        b_last = b_scr[pl.ds(chunk - 1, 1), :]
        qd = (q * jnp.exp2(b)).astype(BF16)
        kd = jnp.exp2(b_last - c).astype(BF16)
        s_decay = jnp.exp2(b_last)

        slabs = []
        for r in range(sub):
            parts = []
            for i in range(n_sub):
                c_row = c_scr[pl.ds(i * sub + r, 1), :]
                lk_row = lk_scr[pl.ds(i * sub + r, 1), :]
                sl = slice(i * sub, (i + 1) * sub)
                parts.append(q[sl] * jnp.exp2(jnp.minimum(b[sl] - c_row, lk_row)))
            slabs.append(_cat(parts, 0).astype(BF16))
        xcat = _cat([_cat([s[:, hs[h]] for s in slabs], 1) for h in range(hp)], 0)
        a_all = jnp.where(diag_mask_all, jnp.dot(xcat, wsel, preferred_element_type=F32), 0.0)

        if n_sub > 1:
            starts = [None] + [b_scr[pl.ds(i * sub - 1, 1), :] for i in range(1, n_sub)]
            ends = [b_scr[pl.ds(i * sub + sub - 1, 1), :] for i in range(n_sub)]
            qh, kh = [], []
            for i in range(n_sub):
                sl = slice(i * sub, (i + 1) * sub)
                qh.append(q[sl] if i == 0 else q[sl] * jnp.exp2(b[sl] - starts[i]))
                kh.append(jnp.exp2(ends[i] - c[sl]))
            khat = _cat(kh, 0).astype(BF16)
            qds = []
            for d in range(1, n_sub):
                parts = []
                for i in range(n_sub):
                    if i < d:
                        parts.append(jnp.zeros((sub, hp * HEAD_DIM), F32))
                    elif d == 1:
                        parts.append(qh[i])
                    else:
                        parts.append(qh[i] * jnp.exp2(starts[i] - ends[i - d]))
                qds.append(_cat(parts, 0).astype(BF16))
            qstack = _cat(qds, 0)

        outs = []
        for h in range(hp):
            st = st_ref[h]
            o = lax.dot_general(qd[:, hs[h]], st.astype(BF16), _NT, preferred_element_type=F32)
            st_ref[h] = st * s_decay[:, hs[h]] + lax.dot_general(vb[:, hs[h]], kd[:, hs[h]], _TN,
                                                                 preferred_element_type=F32)
            a = a_all[h * chunk:(h + 1) * chunk]
            if n_sub > 1:
                m = lax.dot_general(qstack[:, hs[h]], khat[:, hs[h]], _NT, preferred_element_type=F32)
                for d in range(1, n_sub):
                    a = a + jnp.where(blk_diff == d, m[(d - 1) * chunk:d * chunk], 0.0)
            o = o + jnp.dot(a.astype(BF16), vb[:, hs[h]], preferred_element_type=F32)
            ms = jnp.mean(o * o, axis=-1, keepdims=True)
            outs.append((o * lax.rsqrt(ms + NORM_EPS)) * nw[:, hs[h]])
        g = hg * _sigmoid(hg)
        o_ref[pl.ds(r0, chunk), :] = (_cat(outs, 1) * g).astype(o_ref.dtype)
        return carry

    lax.fori_loop(0, tb // chunk, chunk_body, 0)

    @pl.when(ti == pl.num_programs(2) - 1)
    def _():
        for h in range(hp):
            sout_ref[0, h] = st_ref[h].T


def _hgrn_consts(chunk, sub):
    t = np.arange(chunk)
    tri = (t[None, :] <= t[:, None]).astype(np.float32)
    rows = np.arange(sub * HEAD_DIM) // HEAD_DIM
    wsel = (rows[:, None] == (t[None, :] % sub)).astype(np.float32)
    return jnp.asarray(tri, BF16), jnp.asarray(wsel, BF16)


def _hgrn(proj, lb_logits, norm_w, state, batch, seq, heads, col0, hp, tb):
    chunk = min(HGRN_CHUNK, seq)
    sub = min(HGRN_SUB, chunk)
    tri, wsel = _hgrn_consts(chunk, sub)
    has_state = state is not None
    nt = seq // tb
    w = hp * HEAD_DIM

    def col_spec(g):
        return pl.BlockSpec((tb, w), lambda b, h, t: (b * nt + t, (col0 + g * heads) // hp + h))

    const = lambda b, h, t: (0, 0)
    in_specs = [col_spec(0), col_spec(1), col_spec(2), col_spec(3),
                pl.BlockSpec((lb_logits.shape[0], w), lambda b, h, t: (0, h)),
                pl.BlockSpec((1, w), const),
                pl.BlockSpec(tri.shape, const),
                pl.BlockSpec(wsel.shape, const)]
    args = [proj, proj, proj, proj, lb_logits, jnp.tile(norm_w, (1, hp)), tri, wsel]
    if has_state:
        in_specs.append(pl.BlockSpec((1, hp, HEAD_DIM, HEAD_DIM), lambda b, h, t: (b, h, 0, 0)))
        args.append(state)
    return pl.pallas_call(
        functools.partial(_hgrn_kernel, tb=tb, chunk=chunk, sub=sub, hp=hp, has_state=has_state),
        grid=(batch, heads // hp, nt),
        in_specs=in_specs,
        out_specs=[pl.BlockSpec((tb, w), lambda b, h, t: (b * nt + t, h)),
                   pl.BlockSpec((1, hp, HEAD_DIM, HEAD_DIM), lambda b, h, t: (b, h, 0, 0))],
        out_shape=[jax.ShapeDtypeStruct((batch * seq, heads * HEAD_DIM), BF16),
                   jax.ShapeDtypeStruct((batch, heads, HEAD_DIM, HEAD_DIM), F32)],
        scratch_shapes=[pltpu.VMEM((hp, HEAD_DIM, HEAD_DIM), F32),
                        pltpu.VMEM((chunk, w), F32),
                        pltpu.VMEM((chunk, w), F32),
                        pltpu.VMEM((chunk, w), F32)],
        compiler_params=_params(("arbitrary", "arbitrary", "arbitrary")),
        name="hgrn_state" if has_state else "hgrn",
    )(*args)


def _sb_blocks(qs, ks, vs, u2, c, mask):
    tq = qs[0].shape[0]
    z = _cat([lax.dot_general(q, k, _NT, preferred_element_type=F32) for q, k in zip(qs, ks)], 0)
    nz = -z
    sp = jnp.maximum(nz, 0.0) + jnp.log2(1.0 + jnp.exp2(jnp.minimum(z, nz)))
    l = nz - sp
    if mask is not None:
        l = jnp.where(mask, l, 0.0)
    lh = l.astype(BF16)
    ll = (l - lh.astype(F32)).astype(BF16)
    s = jnp.dot(jnp.concatenate([lh, ll], axis=1), u2, preferred_element_type=F32)
    p = jnp.exp2(s + c - sp)
    if mask is not None:
        p = jnp.where(mask, p, 0.0)
    p = p.astype(BF16)
    pv = _cat([jnp.dot(p[h * tq:(h + 1) * tq], vs[h], preferred_element_type=F32) for h in range(len(qs))], 0)
    return pv, c + (s[:, 0:1] + l[:, 0:1])


def _sb_u2(tk):
    s = np.arange(tk)
    u = (s[:, None] > s[None, :]).astype(np.float32)
    return jnp.asarray(np.concatenate([u, u], axis=0), BF16)


def _sb_prompt_kernel(q_ref, k_ref, v_ref, u_ref, o_ref, ko_ref, vo_ref, acc_ref, c_ref, kb_ref, vb_ref,
                      *, tq, hp, scale):
    i = pl.program_id(2)
    hs = [slice(h * HEAD_DIM, (h + 1) * HEAD_DIM) for h in range(hp)]

    @pl.when(i == 0)
    def _():
        for h in range(hp):
            ko_ref[0, h] = k_ref[:, hs[h]]
            vo_ref[0, h] = v_ref[:, hs[h]]
        kb_ref[...] = k_ref[...].astype(BF16)
        vb_ref[...] = v_ref[...].astype(BF16)

    q = (q_ref[...] * scale).astype(BF16)
    qs = [q[:, hs[h]] for h in range(hp)]
    u2 = u_ref[...]
    row = lax.broadcasted_iota(jnp.int32, (tq, tq), 0)
    col = lax.broadcasted_iota(jnp.int32, (tq, tq), 1)
    mask = _cat([col < row] * hp, 0)

    def kv(j0):
        kb = kb_ref[pl.ds(j0, tq), :]
        vb = vb_ref[pl.ds(j0, tq), :]
        return [kb[:, hs[h]] for h in range(hp)], [vb[:, hs[h]] for h in range(hp)]

    ks, vs = kv(pl.multiple_of(i * tq, tq))
    pv, c = _sb_blocks(qs, ks, vs, u2, jnp.zeros((hp * tq, 1), F32), mask)
    acc_ref[...] = pv
    c_ref[...] = c

    def body(jj, carry):
        ks, vs = kv(pl.multiple_of((i - 1 - jj) * tq, tq))
        pv, c = _sb_blocks(qs, ks, vs, u2, c_ref[...], None)
        acc_ref[...] += pv
        c_ref[...] = c
        return carry

    lax.fori_loop(0, i, body, 0)
    o_ref[...] = _cat([acc_ref[pl.ds(h * tq, tq), :] for h in range(hp)], 1).astype(o_ref.dtype)


def _sb_prompt(proj, batch, seq, heads, col0, tq, hp):
    nq = seq // tq
    w = hp * HEAD_DIM
    u2 = _sb_u2(tq)
    kv_shape = jax.ShapeDtypeStruct((batch, heads, seq, HEAD_DIM), F32)
    return pl.pallas_call(
        functools.partial(_sb_prompt_kernel, tq=tq, hp=hp, scale=HEAD_DIM ** -0.5 * LOG2E),
        grid=(batch, heads // hp, nq),
        in_specs=[pl.BlockSpec((tq, w), lambda b, h, i: (b * nq + i, col0 // hp + h)),
                  pl.BlockSpec((seq, w), lambda b, h, i: (b, (col0 + heads) // hp + h)),
                  pl.BlockSpec((seq, w), lambda b, h, i: (b, (col0 + 2 * heads) // hp + h)),
                  pl.BlockSpec(u2.shape, lambda b, h, i: (0, 0))],
        out_specs=[pl.BlockSpec((tq, w), lambda b, h, i: (b * nq + i, h)),
                   pl.BlockSpec((1, hp, seq, HEAD_DIM), lambda b, h, i: (b, h, 0, 0)),
                   pl.BlockSpec((1, hp, seq, HEAD_DIM), lambda b, h, i: (b, h, 0, 0))],
        out_shape=[jax.ShapeDtypeStruct((batch * seq, heads * HEAD_DIM), BF16), kv_shape, kv_shape],
        scratch_shapes=[pltpu.VMEM((hp * tq, HEAD_DIM), F32), pltpu.VMEM((hp * tq, 1), F32),
                        pltpu.VMEM((seq, w), BF16), pltpu.VMEM((seq, w), BF16)],
        compiler_params=_params(("arbitrary", "arbitrary", "arbitrary")),
        name="sb_prompt",
    )(proj, proj, proj, u2)


def _sb_decode_kernel(q_ref, k_ref, v_ref, pk_ref, pv_ref, un_ref, up_ref, o_ref, acc_ref, c_ref,
                      *, seq, heads, tk, scale):
    j = pl.program_id(1)
    hs = [slice(h * HEAD_DIM, (h + 1) * HEAD_DIM) for h in range(heads)]
    q = (q_ref[...] * scale).astype(BF16)
    qs = [q[:, hs[h]] for h in range(heads)]

    @pl.when(j == 0)
    def _():
        pad = jnp.zeros((LANES - seq, heads * HEAD_DIM), F32)
        kn = jnp.concatenate([k_ref[...], pad], axis=0).astype(BF16)
        vn = jnp.concatenate([v_ref[...], pad], axis=0).astype(BF16)
        row = lax.broadcasted_iota(jnp.int32, (seq, LANES), 0)
        col = lax.broadcasted_iota(jnp.int32, (seq, LANES), 1)
        mask = _cat([col < row] * heads, 0)
        pv, c = _sb_blocks(qs, [kn[:, hs[h]] for h in range(heads)], [vn[:, hs[h]] for h in range(heads)],
                           un_ref[...], jnp.zeros((heads * seq, 1), F32), mask)
        acc_ref[...] = pv
        c_ref[...] = c

    u2 = up_ref[...]
    for sblk in reversed(range(pk_ref.shape[2] // tk)):
        ks = [pk_ref[0, h, pl.ds(sblk * tk, tk), :].astype(BF16) for h in range(heads)]
        vs = [pv_ref[0, h, pl.ds(sblk * tk, tk), :].astype(BF16) for h in range(heads)]
        pv, c = _sb_blocks(qs, ks, vs, u2, c_ref[...], None)
        acc_ref[...] += pv
        c_ref[...] = c

    @pl.when(j == pl.num_programs(1) - 1)
    def _():
        o_ref[...] = _cat([acc_ref[pl.ds(h * seq, seq), :] for h in range(heads)], 1).astype(o_ref.dtype)


def _sb_decode(proj, past_k, past_v, batch, seq, heads, col0, tk, tkb):
    past = past_k.shape[2]
    nblk = past // tkb
    w = heads * HEAD_DIM
    un, up = _sb_u2(LANES), _sb_u2(tk)
    kv_spec = pl.BlockSpec((1, heads, tkb, HEAD_DIM), lambda b, j: (b, 0, nblk - 1 - j, 0))
    return pl.pallas_call(
        functools.partial(_sb_decode_kernel, seq=seq, heads=heads, tk=tk, scale=HEAD_DIM ** -0.5 * LOG2E),
        grid=(batch, nblk),
        in_specs=[pl.BlockSpec((seq, w), lambda b, j: (b, col0 // heads)),
                  pl.BlockSpec((seq, w), lambda b, j: (b, col0 // heads + 1)),
                  pl.BlockSpec((seq, w), lambda b, j: (b, col0 // heads + 2)),
                  kv_spec, kv_spec,
                  pl.BlockSpec(un.shape, lambda b, j: (0, 0)),
                  pl.BlockSpec(up.shape, lambda b, j: (0, 0))],
        out_specs=pl.BlockSpec((seq, w), lambda b, j: (b, 0)),
        out_shape=jax.ShapeDtypeStruct((batch * seq, w), BF16),
        scratch_shapes=[pltpu.VMEM((heads * seq, HEAD_DIM), F32), pltpu.VMEM((heads * seq, 1), F32)],
        compiler_params=_params(("arbitrary", "arbitrary")),
        name="sb_decode",
    )(proj, proj, proj, past_k, past_v, un, up)


def _route(logits):
    lane = lax.broadcasted_iota(jnp.int32, logits.shape, 1).astype(F32)
    first = lambda hit: jnp.min(jnp.where(hit, lane, float(LANES)), axis=-1, keepdims=True)
    is_g = lane < N_GROUPS
    gl = jnp.where(is_g, logits, NEG_BIG)
    gmax = jnp.max(gl, axis=-1, keepdims=True)
    gidx = first(gl == gmax)
    gsum = jnp.sum(jnp.where(is_g, jnp.exp(gl - gmax), 0.0), axis=-1, keepdims=True)
    gw = 1.0 / gsum
    lo = N_GROUPS + EXPERTS_PER_GROUP * gidx
    in_g = (lane >= lo) & (lane < lo + EXPERTS_PER_GROUP)
    el = jnp.where(in_g, logits, NEG_BIG)
    m1 = jnp.max(el, axis=-1, keepdims=True)
    i1 = first(in_g & (el == m1))
    rest = in_g & (lane != i1)
    el2 = jnp.where(rest, logits, NEG_BIG)
    m2 = jnp.max(el2, axis=-1, keepdims=True)
    i2 = first(rest & (el2 == m2))
    t = jnp.exp(m2 - m1)
    w0 = gw / (1.0 + t)
    w1 = gw * t / (1.0 + t)
    e0 = i1 - N_GROUPS
    e1 = i2 - N_GROUPS
    return jnp.where(lane == 0, e0, jnp.where(lane == 1, e1, jnp.where(lane == 2, w0, jnp.where(lane == 3, w1, 0.0))))


def _outproj_kernel(x_ref, a_ref, b_ref, wa_ref, wb_ref, nw_ref, rw_ref, rb_ref, *rest):
    x2_ref, xn_ref, r_ref = rest[-3:]
    acc = jnp.dot(a_ref[...], wa_ref[...], preferred_element_type=F32)
    acc = acc + jnp.dot(b_ref[...], wb_ref[...], preferred_element_type=F32)
    x2 = x_ref[...] + acc
    x2_ref[...] = x2
    ms = jnp.mean(x2 * x2, axis=-1, keepdims=True)
    xn = (x2 * lax.rsqrt(ms + NORM_EPS)) * nw_ref[...]
    xn_ref[...] = xn
    xh = xn.astype(BF16)
    xl = (xn - xh.astype(F32)).astype(BF16)
    wh = rw_ref[0]
    logits = (jnp.dot(xh, wh, preferred_element_type=F32) + jnp.dot(xh, rw_ref[1], preferred_element_type=F32)
              + jnp.dot(xl, wh, preferred_element_type=F32)) + rb_ref[...]
    r_ref[...] = _route(logits)


def _outproj(x, oa, ob, wa, wb, nw, rw, rb, tm, n_total, row0, prev):
    n, d = x.shape
    blk0 = row0 // tm
    row = lambda i: (i, 0)
    orow = lambda i: (blk0 + i, 0)
    const = lambda i: (0, 0)
    in_specs = [pl.BlockSpec((tm, d), row),
                pl.BlockSpec((tm, oa.shape[1]), row),
                pl.BlockSpec((tm, ob.shape[1]), row),
                pl.BlockSpec(wa.shape, const),
                pl.BlockSpec(wb.shape, const),
                pl.BlockSpec((1, d), const),
                pl.BlockSpec(rw.shape, lambda i: (0, 0, 0)),
                pl.BlockSpec((1, LANES), const)]
    args = [x, oa, ob, wa, wb, nw, rw, rb]
    aliases = {}
    if prev is not None:
        in_specs += [pl.BlockSpec(memory_space=pl.ANY)] * 3
        aliases = {len(args) + t: t for t in range(3)}
        args += list(prev)
    return pl.pallas_call(
        _outproj_kernel,
        grid=(n // tm,),
        in_specs=in_specs,
        out_specs=[pl.BlockSpec((tm, d), orow), pl.BlockSpec((tm, d), orow), pl.BlockSpec((tm, LANES), orow)],
        out_shape=[jax.ShapeDtypeStruct((n_total, d), F32), jax.ShapeDtypeStruct((n_total, d), F32),
                   jax.ShapeDtypeStruct((n_total, LANES), F32)],
        input_output_aliases=aliases,
        compiler_params=_params(("arbitrary",)),
        name="outproj",
    )(*args)


def _row_out(x_ref, r, dst_hbm, row, sem):
    return pltpu.make_async_copy(x_ref.at[pl.ds(r, 1), :], dst_hbm.at[pl.ds(row, 1), :], sem)


def _scatter_kernel(pos_ref, x_ref, xs_hbm, sem, *, tm):
    def issue(r, carry):
        _row_out(x_ref, r, xs_hbm, pos_ref[0, 0, 2 * r], sem).start()
        _row_out(x_ref, r, xs_hbm, pos_ref[0, 0, 2 * r + 1], sem).start()
        return carry
    lax.fori_loop(0, tm, issue, 0, unroll=8)

    def drain(r, carry):
        _row_out(x_ref, r, xs_hbm, 0, sem).wait()
        _row_out(x_ref, r, xs_hbm, 0, sem).wait()
        return carry
    lax.fori_loop(0, tm, drain, 0, unroll=8)


def _scatter(pos, x, tm):
    n, d = x.shape
    return pl.pallas_call(
        functools.partial(_scatter_kernel, tm=tm),
        grid=(n // tm,),
        in_specs=[pl.BlockSpec((1, 1, 2 * tm), lambda m: (m, 0, 0), memory_space=pltpu.SMEM),
                  pl.BlockSpec((tm, d), lambda m: (m, 0))],
        out_specs=pl.BlockSpec(memory_space=pl.ANY),
        out_shape=jax.ShapeDtypeStruct((2 * n, d), x.dtype),
        scratch_shapes=[pltpu.SemaphoreType.DMA(())],
        compiler_params=_params(("arbitrary",)),
        name="scatter",
    )(pos, x)


def _experts_kernel(it_ref, ie_ref, lo_ref, hi_ref, ni_ref, x_ref, wg_ref, wu_ref, wd_ref, o_ref,
                    g_scr, u_scr, d_scr):
    w = pl.program_id(0)

    @pl.when(w < ni_ref[0])
    def _():
        prev = jnp.maximum(w - 1, 0)
        new_expert = (w == 0) | (ie_ref[w] != ie_ref[prev])
        new_tile = (w == 0) | (it_ref[w] != it_ref[prev])

        @pl.when(new_expert)
        def _():
            g_scr[...] = wg_ref[0].astype(BF16)
            u_scr[...] = wu_ref[0].astype(BF16)
            d_scr[...] = wd_ref[0].astype(BF16)

        x = x_ref[...].astype(BF16)
        g = jnp.dot(x, g_scr[...], preferred_element_type=F32)
        u = jnp.dot(x, u_scr[...], preferred_element_type=F32)
        rows = lax.broadcasted_iota(jnp.int32, g.shape, 0)
        mine = (rows >= lo_ref[w]) & (rows < hi_ref[w])
        hid = jnp.where(mine, (g * _sigmoid(g)) * u, 0.0).astype(BF16)
        y = jnp.dot(hid, d_scr[...], preferred_element_type=F32)

        @pl.when(new_tile)
        def _():
            o_ref[...] = y

        @pl.when(jnp.logical_not(new_tile))
        def _():
            o_ref[...] += y


def _experts(item_tile, item_expert, item_lo, item_hi, n_items, xs, wg, wu, wd, tm):
    rows, d = xs.shape
    ff = wg.shape[2]
    items = item_tile.shape[0]
    tile = lambda w, it, ie, lo, hi, ni: (it[w], 0)
    wsel = lambda w, it, ie, lo, hi, ni: (ie[w], 0, 0)
    return pl.pallas_call(
        _experts_kernel,
        grid_spec=pltpu.PrefetchScalarGridSpec(
            num_scalar_prefetch=5,
            grid=(items,),
            in_specs=[pl.BlockSpec((tm, d), tile),
                      pl.BlockSpec((1, d, ff), wsel),
                      pl.BlockSpec((1, d, ff), wsel),
                      pl.BlockSpec((1, ff, d), wsel)],
            out_specs=pl.BlockSpec((tm, d), tile),
            scratch_shapes=[pltpu.VMEM((d, ff), BF16), pltpu.VMEM((d, ff), BF16), pltpu.VMEM((ff, d), BF16)]),
        out_shape=jax.ShapeDtypeStruct((rows, d), F32),
        compiler_params=_params(("arbitrary",)),
        name="experts",
    )(item_tile, item_expert, item_lo, item_hi, n_items, xs, wg, wu, wd)


def _row_in(src_hbm, row, dst, r, sem):
    return pltpu.make_async_copy(src_hbm.at[pl.ds(row, 1), :], dst.at[pl.ds(r, 1), :], sem)


def _combine_kernel(pos_ref, nxt_ref, x_ref, r_ref, fw_ref, ys_hbm, o_ref, buf, sem, *, tm, ntiles):
    m = pl.program_id(0)
    slot = m % 2

    def issue(tbl, s):
        def go(r, carry):
            _row_in(ys_hbm, tbl[0, 0, 2 * r], buf.at[s, 0], r, sem.at[s]).start()
            _row_in(ys_hbm, tbl[0, 0, 2 * r + 1], buf.at[s, 1], r, sem.at[s]).start()
            return carry
        lax.fori_loop(0, tm, go, 0, unroll=8)

    @pl.when(m == 0)
    def _():
        issue(pos_ref, 0)

    @pl.when(m + 1 < ntiles)
    def _():
        issue(nxt_ref, 1 - slot)

    def drain(r, carry):
        _row_in(ys_hbm, 0, buf.at[slot, 0], r, sem.at[slot]).wait()
        _row_in(ys_hbm, 0, buf.at[slot, 1], r, sem.at[slot]).wait()
        return carry
    lax.fori_loop(0, tm, drain, 0, unroll=8)
    rt = r_ref[...]
    x = x_ref[...] + (buf[slot, 0] * rt[:, 2:3] + buf[slot, 1] * rt[:, 3:4])
    ms = jnp.mean(x * x, axis=-1, keepdims=True)
    o_ref[...] = (x * lax.rsqrt(ms + NORM_EPS)) * fw_ref[...]


def _combine(pos, x2, r, fw, ys, tm, row0, n):
    d = x2.shape[1]
    blk0 = row0 // tm
    ntiles = n // tm
    smem = functools.partial(pl.BlockSpec, memory_space=pltpu.SMEM)
    return pl.pallas_call(
        functools.partial(_combine_kernel, tm=tm, ntiles=ntiles),
        grid=(ntiles,),
        in_specs=[smem((1, 1, 2 * tm), lambda m: (blk0 + m, 0, 0)),
                  smem((1, 1, 2 * tm), lambda m: (blk0 + jnp.minimum(m + 1, ntiles - 1), 0, 0)),
                  pl.BlockSpec((tm, d), lambda m: (blk0 + m, 0)),
                  pl.BlockSpec((tm, LANES), lambda m: (blk0 + m, 0)),
                  pl.BlockSpec((1, d), lambda m: (0, 0)),
                  pl.BlockSpec(memory_space=pl.ANY)],
        out_specs=pl.BlockSpec((tm, d), lambda m: (m, 0)),
        out_shape=jax.ShapeDtypeStruct((n, d), F32),
        scratch_shapes=[pltpu.VMEM((2, 2, tm, d), F32), pltpu.SemaphoreType.DMA((2,))],
        compiler_params=_params(("arbitrary",)),
        name="combine",
    )(pos, pos, x2, r, fw, ys)


def _dispatch_tables(r, tm):
    n = r.shape[0]
    ntiles = (2 * n) // tm
    items = ntiles + N_EXPERTS - 1
    ids = jnp.arange(N_EXPERTS, dtype=jnp.int32)
    e = r[:, 0:2].astype(jnp.int32).reshape(-1)
    onehot = (e[:, None] == ids[None, :]).astype(jnp.int32)
    incl = jnp.cumsum(onehot, axis=0)
    counts = incl[-1]
    cend = jnp.cumsum(counts)
    cstart = cend - counts
    pos = jnp.sum(onehot * (incl - 1 + cstart[None, :]), axis=1)
    first_tile = cstart // tm
    n_it = jnp.where(counts > 0, (cend - 1) // tm - first_tile + 1, 0)
    it_end = jnp.cumsum(n_it)
    it_start = it_end - n_it
    n_items = it_end[-1]
    w = jnp.minimum(jnp.arange(items, dtype=jnp.int32), n_items - 1)
    ie = jnp.sum((it_end[None, :] <= w[:, None]).astype(jnp.int32), axis=1)
    sel = (ie[:, None] == ids[None, :]).astype(jnp.int32)
    pick = lambda tbl: jnp.sum(sel * tbl[None, :], axis=1)
    itile = pick(first_tile) + (w - pick(it_start))
    lo = jnp.maximum(pick(cstart) - itile * tm, 0)
    hi = jnp.minimum(pick(cend) - itile * tm, tm)
    i32 = lambda a: a.astype(jnp.int32)
    return i32(pos), i32(itile), i32(ie), i32(lo), i32(hi), i32(n_items).reshape(1)


def kernel(x_prompt, x_sample, cache_sb_k, cache_sb_v, state_hgrn, w_in, hg_lb_logits, hg_norm_w, w_out, norm1_w,
           norm2_w, router_group_w, router_group_b, router_expert_w, router_expert_b, expert_w_gate, expert_w_up,
           expert_w_down, final_norm_w):
    bp, tp, d = x_prompt.shape
    bs, ts, _ = x_sample.shape
    np_, ns = bp * tp, bs * ts
    n = np_ + ns
    heads = w_in.shape[2] // (7 * HEAD_DIM)
    hw = heads * HEAD_DIM
    tm_tok = 256
    tm_moe = 256

    w_in_b = w_in[0].astype(BF16)
    w_out_b = w_out[0].astype(BF16)
    n1 = norm1_w[0].reshape(1, d)
    n2 = norm2_w[0].reshape(1, d)
    hnw = hg_norm_w[0].reshape(1, HEAD_DIM)
    fw = final_norm_w.reshape(1, d)

    xp = x_prompt.reshape(np_, d)
    xs = x_sample.reshape(ns, d)
    proj_p = _inproj(xp, n1, w_in_b, 1024, 512)
    proj_s = _inproj(xs, n1, w_in_b, ns, 512)

    ohg_p, st_p = _hgrn(proj_p, hg_lb_logits, hnw, None, bp, tp, heads, 0, 4, 512)
    ohg_s, st_s = _hgrn(proj_s, hg_lb_logits, hnw, state_hgrn[0], bs, ts, heads, 0, heads, ts)
    osb_p, k_p, v_p = _sb_prompt(proj_p, bp, tp, heads, 4 * heads, 256, 2)
    osb_s = _sb_decode(proj_s, cache_sb_k[0], cache_sb_v[0], bs, ts, heads, 4 * heads, 512, 1024)

    rw = jnp.concatenate([router_group_w[0], router_expert_w[0]], axis=1)
    rw = jnp.pad(rw, ((0, 0), (0, LANES - rw.shape[1])))
    rwh = rw.astype(BF16)
    rw2 = jnp.stack([rwh, (rw - rwh.astype(F32)).astype(BF16)])
    rb = jnp.concatenate([router_group_b[0], router_expert_b[0]])
    rb = jnp.pad(rb, (0, LANES - rb.shape[0])).reshape(1, LANES)

    wa, wb = w_out_b[:hw], w_out_b[hw:]
    outs = _outproj(xp, ohg_p, osb_p, wa, wb, n2, rw2, rb, tm_tok, n, 0, None)
    x2, xn2, r = _outproj(xs, ohg_s, osb_s, wa, wb, n2, rw2, rb, tm_tok, n, np_, outs)

    pos, item_tile, item_expert, item_lo, item_hi, n_items = _dispatch_tables(r, tm_moe)
    pos = pos.reshape(n // tm_tok, 1, 2 * tm_tok)
    xg = _scatter(pos, xn2, tm_tok)
    ys = _experts(item_tile, item_expert, item_lo, item_hi, n_items, xg,
                  expert_w_gate[0], expert_w_up[0], expert_w_down[0], tm_moe)
    y_p = _combine(pos, x2, r, fw, ys, tm_tok, 0, np_)
    y_s = _combine(pos, x2, r, fw, ys, tm_tok, np_, ns)

    heads_s = lambda a: a.reshape(bs, ts, heads, HEAD_DIM).transpose(0, 2, 1, 3)[None]
    k_s = heads_s(proj_s[:, 5 * hw:6 * hw])
    v_s = heads_s(proj_s[:, 6 * hw:7 * hw])
    return (y_p.reshape(bp, tp, d), y_s.reshape(bs, ts, d), k_p[None], v_p[None], st_p[None],
            k_s, v_s, st_s[None])
```

```python
import functools

import jax
import jax.numpy as jnp
import numpy as np
from jax import lax
from jax.experimental import pallas as pl
from jax.experimental.pallas import tpu as pltpu

F32 = jnp.float32
BF16 = jnp.bfloat16
NORM_EPS = 1e-6
LANES = 128
HEAD_DIM = 128
HGRN_CHUNK = 64
HGRN_SUB = 16
N_GROUPS = 4
EXPERTS_PER_GROUP = 8
N_EXPERTS = N_GROUPS * EXPERTS_PER_GROUP
VMEM_LIMIT = 48 * 1024 * 1024
NEG_BIG = -1e30
LOG2E = 1.4426950408889634

_NT = (((1,), (1,)), ((), ()))
_TN = (((0,), (0,)), ((), ()))


def _params(sem):
    return pltpu.CompilerParams(dimension_semantics=sem, vmem_limit_bytes=VMEM_LIMIT)


def _sigmoid(x):
    return 1.0 / (1.0 + jnp.exp(-x))


def _split3(x):
    h = x.astype(BF16)
    r = x - h.astype(F32)
    m = r.astype(BF16)
    l = (r - m.astype(F32)).astype(BF16)
    return h, m, l


def _cat(parts, axis):
    return parts[0] if len(parts) == 1 else jnp.concatenate(parts, axis=axis)


def _inproj_kernel(x_ref, nw_ref, w_ref, o_ref):
    x = x_ref[...]
    ms = jnp.mean(x * x, axis=-1, keepdims=True)
    xn = ((x * lax.rsqrt(ms + NORM_EPS)) * nw_ref[...]).astype(BF16)
    o_ref[...] = jnp.dot(xn, w_ref[...], preferred_element_type=F32)


def _inproj(x, nw, w, tm, tn):
    n, d = x.shape
    c = w.shape[1]
    return pl.pallas_call(
        _inproj_kernel,
        grid=(c // tn, n // tm),
        in_specs=[pl.BlockSpec((tm, d), lambda j, i: (i, 0)),
                  pl.BlockSpec((1, d), lambda j, i: (0, 0)),
                  pl.BlockSpec((d, tn), lambda j, i: (0, j))],
        out_specs=pl.BlockSpec((tm, tn), lambda j, i: (i, j)),
        out_shape=jax.ShapeDtypeStruct((n, c), F32),
        compiler_params=_params(("arbitrary", "arbitrary")),
        name="inproj",
    )(x, nw, w)


def _hgrn_kernel(*refs, tb, chunk, sub, hp, has_state):
    if has_state:
        (hq_ref, hf_ref, hi_ref, hg_ref, lbl_ref, nw_ref, tri_ref, wsel_ref, s0_ref,
         o_ref, sout_ref, st_ref, b_scr, c_scr) = refs
    else:
        (hq_ref, hf_ref, hi_ref, hg_ref, lbl_ref, nw_ref, tri_ref, wsel_ref,
         o_ref, sout_ref, st_ref, b_scr, c_scr) = refs
    n_sub = chunk // sub
    ti = pl.program_id(2)
    hs = [slice(h * HEAD_DIM, (h + 1) * HEAD_DIM) for h in range(hp)]

    lg = lbl_ref[...]
    lg = jnp.exp(lg - jnp.max(lg, axis=0, keepdims=True))
    lb = lg[0:1, :] / jnp.sum(lg, axis=0, keepdims=True)

    @pl.when(ti == 0)
    def _():
        for h in range(hp):
            st_ref[h] = s0_ref[0, h].T if has_state else jnp.zeros((HEAD_DIM, HEAD_DIM), F32)

    row = lax.broadcasted_iota(jnp.int32, (chunk, chunk), 0)
    col = lax.broadcasted_iota(jnp.int32, (chunk, chunk), 1)
    blk_diff = row // sub - col // sub
    diag_mask = (blk_diff == 0) & (col <= row)
    diag_mask_all = _cat([diag_mask] * hp, 0)
    tri = tri_ref[...]
    wsel = wsel_ref[...]
    nw = nw_ref[...]

    def chunk_body(ci, carry):
        r0 = pl.multiple_of(ci * chunk, chunk)
        hq = hq_ref[pl.ds(r0, chunk), :]
        hf = hf_ref[pl.ds(r0, chunk), :]
        v = hi_ref[pl.ds(r0, chunk), :]
        hg = hg_ref[pl.ds(r0, chunk), :]
        q = hq * _sigmoid(hq)
        f = lb + (1.0 - lb) * _sigmoid(hf)
        k = 1.0 - f
        lf = jnp.log(f) * LOG2E
        l1, l2, l3 = _split3(lf)
        b = (jnp.dot(tri, l1, preferred_element_type=F32) + jnp.dot(tri, l2, preferred_element_type=F32)
             + jnp.dot(tri, l3, preferred_element_type=F32))
        c = b - jnp.log(k) * LOG2E
        b_scr[...] = b
        c_scr[...] = c
        vb = v.astype(BF16)
        b_last = b_scr[pl.ds(chunk - 1, 1), :]
        qd = (q * jnp.exp2(b)).astype(BF16)
        kd = jnp.exp2(b_last - c).astype(BF16)
        s_decay = jnp.exp2(b_last)

        slabs = []
        for r in range(sub):
            parts = []
            for i in range(n_sub):
                c_row = c_scr[pl.ds(i * sub + r, 1), :]
                sl = slice(i * sub, (i + 1) * sub)
                parts.append(q[sl] * jnp.exp2(jnp.minimum(b[sl] - c_row, 0.0)))
            slabs.append(_cat(parts, 0).astype(BF16))
        xcat = _cat([_cat([s[:, hs[h]] for s in slabs], 1) for h in range(hp)], 0)
        a_all = jnp.where(diag_mask_all, jnp.dot(xcat, wsel, preferred_element_type=F32), 0.0)

        if n_sub > 1:
            starts = [None] + [b_scr[pl.ds(i * sub - 1, 1), :] for i in range(1, n_sub)]
            ends = [b_scr[pl.ds(i * sub + sub - 1, 1), :] for i in range(n_sub)]
            qh, kh = [], []
            for i in range(n_sub):
                sl = slice(i * sub, (i + 1) * sub)
                qh.append(q[sl] if i == 0 else q[sl] * jnp.exp2(b[sl] - starts[i]))
                kh.append(jnp.exp2(ends[i] - c[sl]))
            khat = _cat(kh, 0).astype(BF16)
            qds = []
            for d in range(1, n_sub):
                parts = []
                for i in range(n_sub):
                    if i < d:
                        parts.append(jnp.zeros((sub, hp * HEAD_DIM), F32))
                    elif d == 1:
                        parts.append(qh[i])
                    else:
                        parts.append(qh[i] * jnp.exp2(starts[i] - ends[i - d]))
                qds.append(_cat(parts, 0).astype(BF16))
            qstack = _cat(qds, 0)

        outs = []
        for h in range(hp):
            st = st_ref[h]
            o = lax.dot_general(qd[:, hs[h]], st.astype(BF16), _NT, preferred_element_type=F32)
            st_ref[h] = st * s_decay[:, hs[h]] + lax.dot_general(vb[:, hs[h]], kd[:, hs[h]], _TN,
                                                                 preferred_element_type=F32)
            a = a_all[h * chunk:(h + 1) * chunk]
            if n_sub > 1:
                m = lax.dot_general(qstack[:, hs[h]], khat[:, hs[h]], _NT, preferred_element_type=F32)
                for d in range(1, n_sub):
                    a = a + jnp.where(blk_diff == d, m[(d - 1) * chunk:d * chunk], 0.0)
            o = o + jnp.dot(a.astype(BF16), vb[:, hs[h]], preferred_element_type=F32)
            ms = jnp.mean(o * o, axis=-1, keepdims=True)
            outs.append((o * lax.rsqrt(ms + NORM_EPS)) * nw[:, hs[h]])
        g = hg * _sigmoid(hg)
        o_ref[pl.ds(r0, chunk), :] = (_cat(outs, 1) * g).astype(o_ref.dtype)
        return carry

    lax.fori_loop(0, tb // chunk, chunk_body, 0)

    @pl.when(ti == pl.num_programs(2) - 1)
    def _():
        for h in range(hp):
            sout_ref[0, h] = st_ref[h].T


def _hgrn_consts(chunk, sub):
    t = np.arange(chunk)
    tri = (t[None, :] <= t[:, None]).astype(np.float32)
    rows = np.arange(sub * HEAD_DIM) // HEAD_DIM
    wsel = (rows[:, None] == (t[None, :] % sub)).astype(np.float32)
    return jnp.asarray(tri, BF16), jnp.asarray(wsel, BF16)


def _hgrn(proj, lb_logits, norm_w, state, batch, seq, heads, col0, hp, tb):
    chunk = min(HGRN_CHUNK, seq)
    sub = min(HGRN_SUB, chunk)
    tri, wsel = _hgrn_consts(chunk, sub)
    has_state = state is not None
    nt = seq // tb
    w = hp * HEAD_DIM

    def col_spec(g):
        return pl.BlockSpec((tb, w), lambda b, h, t: (b * nt + t, (col0 + g * heads) // hp + h))

    const = lambda b, h, t: (0, 0)
    in_specs = [col_spec(0), col_spec(1), col_spec(2), col_spec(3),
                pl.BlockSpec((lb_logits.shape[0], w), lambda b, h, t: (0, h)),
                pl.BlockSpec((1, w), const),
                pl.BlockSpec(tri.shape, const),
                pl.BlockSpec(wsel.shape, const)]
    args = [proj, proj, proj, proj, lb_logits, jnp.tile(norm_w, (1, hp)), tri, wsel]
    if has_state:
        in_specs.append(pl.BlockSpec((1, hp, HEAD_DIM, HEAD_DIM), lambda b, h, t: (b, h, 0, 0)))
        args.append(state)
    return pl.pallas_call(
        functools.partial(_hgrn_kernel, tb=tb, chunk=chunk, sub=sub, hp=hp, has_state=has_state),
        grid=(batch, heads // hp, nt),
        in_specs=in_specs,
        out_specs=[pl.BlockSpec((tb, w), lambda b, h, t: (b * nt + t, h)),
                   pl.BlockSpec((1, hp, HEAD_DIM, HEAD_DIM), lambda b, h, t: (b, h, 0, 0))],
        out_shape=[jax.ShapeDtypeStruct((batch * seq, heads * HEAD_DIM), BF16),
                   jax.ShapeDtypeStruct((batch, heads, HEAD_DIM, HEAD_DIM), F32)],
        scratch_shapes=[pltpu.VMEM((hp, HEAD_DIM, HEAD_DIM), F32),
                        pltpu.VMEM((chunk, w), F32),
                        pltpu.VMEM((chunk, w), F32)],
        compiler_params=_params(("arbitrary", "arbitrary", "arbitrary")),
        name="hgrn_state" if has_state else "hgrn",
    )(*args)


def _sb_blocks(qs, ks, vs, u, c, mask):
    tq = qs[0].shape[0]
    z = _cat([lax.dot_general(q, k, _NT, preferred_element_type=F32) for q, k in zip(qs, ks)], 0)
    nz = -z
    sp = jnp.maximum(nz, 0.0) + jnp.log2(1.0 + jnp.exp2(jnp.minimum(z, nz)))
    l = nz - sp
    if mask is not None:
        l = jnp.where(mask, l, 0.0)
    s = jnp.dot(l.astype(BF16), u, preferred_element_type=F32)
    p = jnp.exp2(s + c - sp)
    if mask is not None:
        p = jnp.where(mask, p, 0.0)
    p = p.astype(BF16)
    pv = _cat([jnp.dot(p[h * tq:(h + 1) * tq], vs[h], preferred_element_type=F32) for h in range(len(qs))], 0)
    return pv, c + (s[:, 0:1] + l[:, 0:1])


def _sb_suffix(tk):
    s = np.arange(tk)
    u = (s[:, None] > s[None, :]).astype(np.float32)
    return jnp.asarray(u, BF16)


def _sb_prompt_kernel(q_ref, k_ref, v_ref, u_ref, o_ref, ko_ref, vo_ref, acc_ref, c_ref, kb_ref, vb_ref,
                      lc_ref, sp_ref, l0_ref, *, tq, hp, scale):
    i = pl.program_id(2)
    hs = [slice(h * HEAD_DIM, (h + 1) * HEAD_DIM) for h in range(hp)]

    @pl.when(i == 0)
    def _():
        for h in range(hp):
            ko_ref[0, h] = k_ref[:, hs[h]]
            vo_ref[0, h] = v_ref[:, hs[h]]
        kb_ref[...] = k_ref[...].astype(BF16)
        vb_ref[...] = v_ref[...].astype(BF16)

    q = (q_ref[...] * scale).astype(BF16)
    usuf = u_ref[...]

    def scores(j, slot, diagonal):
        kb = kb_ref[pl.ds(pl.multiple_of(j * tq, tq), tq), :]
        z = _cat([lax.dot_general(q[:, hs[h]], kb[:, hs[h]], _NT, preferred_element_type=F32) for h in range(hp)], 0)
        nz = -z
        sp = jnp.maximum(nz, 0.0) + jnp.log2(1.0 + jnp.exp2(jnp.minimum(z, nz)))
        l = nz - sp
        if diagonal:
            row = lax.broadcasted_iota(jnp.int32, (tq, tq), 0)
            col = lax.broadcasted_iota(jnp.int32, (tq, tq), 1)
            mask = _cat([col < row] * hp, 0)
            l = jnp.where(mask, l, 0.0)
            sp = jnp.where(mask, sp, -NEG_BIG)
        lc_ref[slot] = l.astype(BF16)
        sp_ref[slot] = sp
        l0_ref[slot] = l[:, 0:1]

    def values(j, slot):
        vb = vb_ref[pl.ds(pl.multiple_of(j * tq, tq), tq), :]
        s = jnp.dot(lc_ref[slot], usuf, preferred_element_type=F32)
        c = c_ref[...]
        p = jnp.exp2(s + c - sp_ref[slot]).astype(BF16)
        acc_ref[...] += _cat([jnp.dot(p[h * tq:(h + 1) * tq], vb[:, hs[h]], preferred_element_type=F32)
                              for h in range(hp)], 0)
        c_ref[...] = c + (s[:, 0:1] + l0_ref[slot])

    acc_ref[...] = jnp.zeros_like(acc_ref)
    c_ref[...] = jnp.zeros_like(c_ref)
    scores(i, 0, True)

    def pair(pp, carry):
        j = i - 1 - 2 * pp
        scores(j, 1, False)
        values(j + 1, 0)
        scores(j - 1, 0, False)
        values(j, 1)
        return carry

    lax.fori_loop(0, i // 2, pair, 0)

    @pl.when(i % 2 == 1)
    def _():
        scores(0, 1, False)
        values(1, 0)
        values(0, 1)

    @pl.when(i % 2 == 0)
    def _():
        values(0, 0)
    o_ref[...] = _cat([acc_ref[pl.ds(h * tq, tq), :] for h in range(hp)], 1).astype(o_ref.dtype)


def _sb_prompt(proj, batch, seq, heads, col0, tq, hp):
    nq = seq // tq
    w = hp * HEAD_DIM
    usuf = _sb_suffix(tq)
    kv_shape = jax.ShapeDtypeStruct((batch, heads, seq, HEAD_DIM), F32)
    return pl.pallas_call(
        functools.partial(_sb_prompt_kernel, tq=tq, hp=hp, scale=HEAD_DIM ** -0.5 * LOG2E),
        grid=(batch, heads // hp, nq),
        in_specs=[pl.BlockSpec((tq, w), lambda b, h, i: (b * nq + i, col0 // hp + h)),
                  pl.BlockSpec((seq, w), lambda b, h, i: (b, (col0 + heads) // hp + h)),
                  pl.BlockSpec((seq, w), lambda b, h, i: (b, (col0 + 2 * heads) // hp + h)),
                  pl.BlockSpec(usuf.shape, lambda b, h, i: (0, 0))],
        out_specs=[pl.BlockSpec((tq, w), lambda b, h, i: (b * nq + i, h)),
                   pl.BlockSpec((1, hp, seq, HEAD_DIM), lambda b, h, i: (b, h, 0, 0)),
                   pl.BlockSpec((1, hp, seq, HEAD_DIM), lambda b, h, i: (b, h, 0, 0))],
        out_shape=[jax.ShapeDtypeStruct((batch * seq, heads * HEAD_DIM), BF16), kv_shape, kv_shape],
        scratch_shapes=[pltpu.VMEM((hp * tq, HEAD_DIM), F32), pltpu.VMEM((hp * tq, 1), F32),
                        pltpu.VMEM((seq, w), BF16), pltpu.VMEM((seq, w), BF16),
                        pltpu.VMEM((2, hp * tq, tq), BF16), pltpu.VMEM((2, hp * tq, tq), F32),
                        pltpu.VMEM((2, hp * tq, 1), F32)],
        compiler_params=_params(("arbitrary", "arbitrary", "arbitrary")),
        name="sb_prompt",
    )(proj, proj, proj, usuf)


def _sb_decode_kernel(q_ref, k_ref, v_ref, pk_ref, pv_ref, un_ref, up_ref, o_ref, acc_ref, c_ref,
                      *, seq, heads, tk, scale):
    j = pl.program_id(1)
    hs = [slice(h * HEAD_DIM, (h + 1) * HEAD_DIM) for h in range(heads)]
    q = (q_ref[...] * scale).astype(BF16)
    qs = [q[:, hs[h]] for h in range(heads)]

    @pl.when(j == 0)
    def _():
        pad = jnp.zeros((LANES - seq, heads * HEAD_DIM), F32)
        kn = jnp.concatenate([k_ref[...], pad], axis=0).astype(BF16)
        vn = jnp.concatenate([v_ref[...], pad], axis=0).astype(BF16)
        row = lax.broadcasted_iota(jnp.int32, (seq, LANES), 0)
        col = lax.broadcasted_iota(jnp.int32, (seq, LANES), 1)
        mask = _cat([col < row] * heads, 0)
        pv, c = _sb_blocks(qs, [kn[:, hs[h]] for h in range(heads)], [vn[:, hs[h]] for h in range(heads)],
                           un_ref[...], jnp.zeros((heads * seq, 1), F32), mask)
        acc_ref[...] = pv
        c_ref[...] = c

    usuf = up_ref[...]
    for sblk in reversed(range(pk_ref.shape[2] // tk)):
        ks = [pk_ref[0, h, pl.ds(sblk * tk, tk), :].astype(BF16) for h in range(heads)]
        vs = [pv_ref[0, h, pl.ds(sblk * tk, tk), :].astype(BF16) for h in range(heads)]
        pv, c = _sb_blocks(qs, ks, vs, usuf, c_ref[...], None)
        acc_ref[...] += pv
        c_ref[...] = c

    @pl.when(j == pl.num_programs(1) - 1)
    def _():
        o_ref[...] = _cat([acc_ref[pl.ds(h * seq, seq), :] for h in range(heads)], 1).astype(o_ref.dtype)


def _sb_decode(proj, past_k, past_v, batch, seq, heads, col0, tk, tkb):
    past = past_k.shape[2]
    nblk = past // tkb
    w = heads * HEAD_DIM
    un, up = _sb_suffix(LANES), _sb_suffix(tk)
    kv_spec = pl.BlockSpec((1, heads, tkb, HEAD_DIM), lambda b, j: (b, 0, nblk - 1 - j, 0))
    return pl.pallas_call(
        functools.partial(_sb_decode_kernel, seq=seq, heads=heads, tk=tk, scale=HEAD_DIM ** -0.5 * LOG2E),
        grid=(batch, nblk),
        in_specs=[pl.BlockSpec((seq, w), lambda b, j: (b, col0 // heads)),
                  pl.BlockSpec((seq, w), lambda b, j: (b, col0 // heads + 1)),
                  pl.BlockSpec((seq, w), lambda b, j: (b, col0 // heads + 2)),
                  kv_spec, kv_spec,
                  pl.BlockSpec(un.shape, lambda b, j: (0, 0)),
                  pl.BlockSpec(up.shape, lambda b, j: (0, 0))],
        out_specs=pl.BlockSpec((seq, w), lambda b, j: (b, 0)),
        out_shape=jax.ShapeDtypeStruct((batch * seq, w), BF16),
        scratch_shapes=[pltpu.VMEM((heads * seq, HEAD_DIM), F32), pltpu.VMEM((heads * seq, 1), F32)],
        compiler_params=_params(("arbitrary", "arbitrary")),
        name="sb_decode",
    )(proj, proj, proj, past_k, past_v, un, up)


def _route(logits):
    lane = lax.broadcasted_iota(jnp.int32, logits.shape, 1).astype(F32)
    first = lambda hit: jnp.min(jnp.where(hit, lane, float(LANES)), axis=-1, keepdims=True)
    is_g = lane < N_GROUPS
    gl = jnp.where(is_g, logits, NEG_BIG)
    gmax = jnp.max(gl, axis=-1, keepdims=True)
    gidx = first(gl == gmax)
    gsum = jnp.sum(jnp.where(is_g, jnp.exp(gl - gmax), 0.0), axis=-1, keepdims=True)
    gw = 1.0 / gsum
    lo = N_GROUPS + EXPERTS_PER_GROUP * gidx
    in_g = (lane >= lo) & (lane < lo + EXPERTS_PER_GROUP)
    el = jnp.where(in_g, logits, NEG_BIG)
    m1 = jnp.max(el, axis=-1, keepdims=True)
    i1 = first(in_g & (el == m1))
    rest = in_g & (lane != i1)
    el2 = jnp.where(rest, logits, NEG_BIG)
    m2 = jnp.max(el2, axis=-1, keepdims=True)
    i2 = first(rest & (el2 == m2))
    t = jnp.exp(m2 - m1)
    w0 = gw / (1.0 + t)
    w1 = gw * t / (1.0 + t)
    e0 = i1 - N_GROUPS
    e1 = i2 - N_GROUPS
    return jnp.where(lane == 0, e0, jnp.where(lane == 1, e1, jnp.where(lane == 2, w0, jnp.where(lane == 3, w1, 0.0))))


def _outproj_kernel(x_ref, a_ref, b_ref, wa_ref, wb_ref, nw_ref, rw_ref, rb_ref, *rest):
    x2_ref, xn_ref, r_ref = rest[-3:]
    acc = jnp.dot(a_ref[...], wa_ref[...], preferred_element_type=F32)
    acc = acc + jnp.dot(b_ref[...], wb_ref[...], preferred_element_type=F32)
    x2 = x_ref[...] + acc
    x2_ref[...] = x2
    ms = jnp.mean(x2 * x2, axis=-1, keepdims=True)
    xn = (x2 * lax.rsqrt(ms + NORM_EPS)) * nw_ref[...]
    xn_ref[...] = xn
    xh = xn.astype(BF16)
    xl = (xn - xh.astype(F32)).astype(BF16)
    wh = rw_ref[0]
    logits = (jnp.dot(xh, wh, preferred_element_type=F32) + jnp.dot(xh, rw_ref[1], preferred_element_type=F32)
              + jnp.dot(xl, wh, preferred_element_type=F32)) + rb_ref[...]
    r_ref[...] = _route(logits)


def _outproj(x, oa, ob, wa, wb, nw, rw, rb, tm, n_total, row0, prev):
    n, d = x.shape
    blk0 = row0 // tm
    row = lambda i: (i, 0)
    orow = lambda i: (blk0 + i, 0)
    const = lambda i: (0, 0)
    in_specs = [pl.BlockSpec((tm, d), row),
                pl.BlockSpec((tm, oa.shape[1]), row),
                pl.BlockSpec((tm, ob.shape[1]), row),
                pl.BlockSpec(wa.shape, const),
                pl.BlockSpec(wb.shape, const),
                pl.BlockSpec((1, d), const),
                pl.BlockSpec(rw.shape, lambda i: (0, 0, 0)),
                pl.BlockSpec((1, LANES), const)]
    args = [x, oa, ob, wa, wb, nw, rw, rb]
    aliases = {}
    if prev is not None:
        in_specs += [pl.BlockSpec(memory_space=pl.ANY)] * 3
        aliases = {len(args) + t: t for t in range(3)}
        args += list(prev)
    return pl.pallas_call(
        _outproj_kernel,
        grid=(n // tm,),
        in_specs=in_specs,
        out_specs=[pl.BlockSpec((tm, d), orow), pl.BlockSpec((tm, d), orow), pl.BlockSpec((tm, LANES), orow)],
        out_shape=[jax.ShapeDtypeStruct((n_total, d), F32), jax.ShapeDtypeStruct((n_total, d), F32),
                   jax.ShapeDtypeStruct((n_total, LANES), F32)],
        input_output_aliases=aliases,
        compiler_params=_params(("arbitrary",)),
        name="outproj",
    )(*args)


def _row_out(x_ref, r, dst_hbm, row, sem):
    return pltpu.make_async_copy(x_ref.at[pl.ds(r, 1), :], dst_hbm.at[pl.ds(row, 1), :], sem)


def _scatter_kernel(pos_ref, x_ref, xs_hbm, sem, *, tm):
    def issue(r, carry):
        _row_out(x_ref, r, xs_hbm, pos_ref[0, 0, 2 * r], sem).start()
        _row_out(x_ref, r, xs_hbm, pos_ref[0, 0, 2 * r + 1], sem).start()
        return carry
    lax.fori_loop(0, tm, issue, 0, unroll=8)

    def drain(r, carry):
        _row_out(x_ref, r, xs_hbm, 0, sem).wait()
        _row_out(x_ref, r, xs_hbm, 0, sem).wait()
        return carry
    lax.fori_loop(0, tm, drain, 0, unroll=8)


def _scatter(pos, x, tm):
    n, d = x.shape
    return pl.pallas_call(
        functools.partial(_scatter_kernel, tm=tm),
        grid=(n // tm,),
        in_specs=[pl.BlockSpec((1, 1, 2 * tm), lambda m: (m, 0, 0), memory_space=pltpu.SMEM),
                  pl.BlockSpec((tm, d), lambda m: (m, 0))],
        out_specs=pl.BlockSpec(memory_space=pl.ANY),
        out_shape=jax.ShapeDtypeStruct((2 * n, d), x.dtype),
        scratch_shapes=[pltpu.SemaphoreType.DMA(())],
        compiler_params=_params(("arbitrary",)),
        name="scatter",
    )(pos, x)


def _weight_copies(e, slot, w_hbm, w_stage, sem):
    return [pltpu.make_async_copy(w_hbm[t].at[e], w_stage[t].at[slot], sem.at[slot, t]) for t in range(3)]


def _experts_kernel(it_ref, ie_ref, lo_ref, hi_ref, ni_ref, nx_ref, ig_ref, x_ref, wg_hbm, wu_hbm, wd_hbm, o_ref,
                    sg, su, sd, g_scr, u_scr, d_scr, sem):
    w = pl.program_id(0)
    w_hbm, w_stage = (wg_hbm, wu_hbm, wd_hbm), (sg, su, sd)

    @pl.when(w < ni_ref[0])
    def _():
        prev = jnp.maximum(w - 1, 0)
        e = ie_ref[w]
        new_expert = (w == 0) | (e != ie_ref[prev])
        new_tile = (w == 0) | (it_ref[w] != it_ref[prev])
        slot = ig_ref[w] % 2

        @pl.when(w == 0)
        def _():
            for cp in _weight_copies(e, slot, w_hbm, w_stage, sem):
                cp.start()

        @pl.when(new_expert)
        def _():
            for cp in _weight_copies(e, slot, w_hbm, w_stage, sem):
                cp.wait()

            @pl.when(nx_ref[w] != e)
            def _():
                for cp in _weight_copies(nx_ref[w], 1 - slot, w_hbm, w_stage, sem):
                    cp.start()

            g_scr[...] = sg[slot].astype(BF16)
            u_scr[...] = su[slot].astype(BF16)
            d_scr[...] = sd[slot].astype(BF16)

        x = x_ref[...].astype(BF16)
        g = jnp.dot(x, g_scr[...], preferred_element_type=F32)
        u = jnp.dot(x, u_scr[...], preferred_element_type=F32)
        rows = lax.broadcasted_iota(jnp.int32, g.shape, 0)
        mine = (rows >= lo_ref[w]) & (rows < hi_ref[w])
        hid = jnp.where(mine, (g * _sigmoid(g)) * u, 0.0).astype(BF16)
        y = jnp.dot(hid, d_scr[...], preferred_element_type=F32)

        @pl.when(new_tile)
        def _():
            o_ref[...] = y

        @pl.when(jnp.logical_not(new_tile))
        def _():
            o_ref[...] += y


def _experts(item_tile, item_expert, item_lo, item_hi, n_items, item_next, item_group, xs, wg, wu, wd, tm):
    rows, d = xs.shape
    ff = wg.shape[2]
    items = item_tile.shape[0]
    tile = lambda w, it, *_: (it[w], 0)
    hbm = pl.BlockSpec(memory_space=pl.ANY)
    return pl.pallas_call(
        _experts_kernel,
        grid_spec=pltpu.PrefetchScalarGridSpec(
            num_scalar_prefetch=7,
            grid=(items,),
            in_specs=[pl.BlockSpec((tm, d), tile), hbm, hbm, hbm],
            out_specs=pl.BlockSpec((tm, d), tile),
            scratch_shapes=[pltpu.VMEM((2, d, ff), F32), pltpu.VMEM((2, d, ff), F32), pltpu.VMEM((2, ff, d), F32),
                            pltpu.VMEM((d, ff), BF16), pltpu.VMEM((d, ff), BF16), pltpu.VMEM((ff, d), BF16),
                            pltpu.SemaphoreType.DMA((2, 3))]),
        out_shape=jax.ShapeDtypeStruct((rows, d), F32),
        compiler_params=_params(("arbitrary",)),
        name="experts",
    )(item_tile, item_expert, item_lo, item_hi, n_items, item_next, item_group, xs, wg, wu, wd)


def _row_in(src_hbm, row, dst, r, sem):
    return pltpu.make_async_copy(src_hbm.at[pl.ds(row, 1), :], dst.at[pl.ds(r, 1), :], sem)


def _combine_kernel(pos_ref, nxt_ref, x_ref, r_ref, fw_ref, ys_hbm, o_ref, buf, sem, *, tm, ntiles):
    m = pl.program_id(0)
    slot = m % 2

    def issue(tbl, s):
        def go(r, carry):
            _row_in(ys_hbm, tbl[0, 0, 2 * r], buf.at[s, 0], r, sem.at[s]).start()
            _row_in(ys_hbm, tbl[0, 0, 2 * r + 1], buf.at[s, 1], r, sem.at[s]).start()
            return carry
        lax.fori_loop(0, tm, go, 0, unroll=8)

    @pl.when(m == 0)
    def _():
        issue(pos_ref, 0)

    @pl.when(m + 1 < ntiles)
    def _():
        issue(nxt_ref, 1 - slot)

    def drain(r, carry):
        _row_in(ys_hbm, 0, buf.at[slot, 0], r, sem.at[slot]).wait()
        _row_in(ys_hbm, 0, buf.at[slot, 1], r, sem.at[slot]).wait()
        return carry
    lax.fori_loop(0, tm, drain, 0, unroll=8)
    rt = r_ref[...]
    x = x_ref[...] + (buf[slot, 0] * rt[:, 2:3] + buf[slot, 1] * rt[:, 3:4])
    ms = jnp.mean(x * x, axis=-1, keepdims=True)
    o_ref[...] = (x * lax.rsqrt(ms + NORM_EPS)) * fw_ref[...]


def _combine(pos, x2, r, fw, ys, tm, row0, n):
    d = x2.shape[1]
    blk0 = row0 // tm
    ntiles = n // tm
    smem = functools.partial(pl.BlockSpec, memory_space=pltpu.SMEM)
    return pl.pallas_call(
        functools.partial(_combine_kernel, tm=tm, ntiles=ntiles),
        grid=(ntiles,),
        in_specs=[smem((1, 1, 2 * tm), lambda m: (blk0 + m, 0, 0)),
                  smem((1, 1, 2 * tm), lambda m: (blk0 + jnp.minimum(m + 1, ntiles - 1), 0, 0)),
                  pl.BlockSpec((tm, d), lambda m: (blk0 + m, 0)),
                  pl.BlockSpec((tm, LANES), lambda m: (blk0 + m, 0)),
                  pl.BlockSpec((1, d), lambda m: (0, 0)),
                  pl.BlockSpec(memory_space=pl.ANY)],
        out_specs=pl.BlockSpec((tm, d), lambda m: (m, 0)),
        out_shape=jax.ShapeDtypeStruct((n, d), F32),
        scratch_shapes=[pltpu.VMEM((2, 2, tm, d), F32), pltpu.SemaphoreType.DMA((2,))],
        compiler_params=_params(("arbitrary",)),
        name="combine",
    )(pos, pos, x2, r, fw, ys)


def _dispatch_tables(r, tm):
    n = r.shape[0]
    ntiles = (2 * n) // tm
    items = ntiles + N_EXPERTS - 1
    ids = jnp.arange(N_EXPERTS, dtype=jnp.int32)
    e = r[:, 0:2].astype(jnp.int32).reshape(-1)
    onehot = (e[:, None] == ids[None, :]).astype(jnp.int32)
    incl = jnp.cumsum(onehot, axis=0)
    counts = incl[-1]
    cend = jnp.cumsum(counts)
    cstart = cend - counts
    pos = jnp.sum(onehot * (incl - 1 + cstart[None, :]), axis=1)
    first_tile = cstart // tm
    n_it = jnp.where(counts > 0, (cend - 1) // tm - first_tile + 1, 0)
    it_end = jnp.cumsum(n_it)
    it_start = it_end - n_it
    n_items = it_end[-1]
    w = jnp.minimum(jnp.arange(items, dtype=jnp.int32), n_items - 1)
    ie = jnp.sum((it_end[None, :] <= w[:, None]).astype(jnp.int32), axis=1)
    sel = (ie[:, None] == ids[None, :]).astype(jnp.int32)
    pick = lambda tbl: jnp.sum(sel * tbl[None, :], axis=1)
    itile = pick(first_tile) + (w - pick(it_start))
    lo = jnp.maximum(pick(cstart) - itile * tm, 0)
    hi = jnp.minimum(pick(cend) - itile * tm, tm)
    used = counts > 0
    later = used[None, :] & (ids[None, :] > ids[:, None])
    next_used = jnp.min(jnp.where(later, ids[None, :], N_EXPERTS), axis=1)
    next_used = jnp.where(next_used == N_EXPERTS, ids, next_used)
    group = jnp.cumsum(used.astype(jnp.int32)) - 1
    i32 = lambda a: a.astype(jnp.int32)
    return (i32(pos), i32(itile), i32(ie), i32(lo), i32(hi), i32(n_items).reshape(1), i32(pick(next_used)),
            i32(pick(group)))


def kernel(x_prompt, x_sample, cache_sb_k, cache_sb_v, state_hgrn, w_in, hg_lb_logits, hg_norm_w, w_out, norm1_w,
           norm2_w, router_group_w, router_group_b, router_expert_w, router_expert_b, expert_w_gate, expert_w_up,
           expert_w_down, final_norm_w):
    bp, tp, d = x_prompt.shape
    bs, ts, _ = x_sample.shape
    np_, ns = bp * tp, bs * ts
    n = np_ + ns
    heads = w_in.shape[2] // (7 * HEAD_DIM)
    hw = heads * HEAD_DIM
    tm_tok = 256
    tm_moe = 256

    w_in_b = w_in[0].astype(BF16)
    w_out_b = w_out[0].astype(BF16)
    n1 = norm1_w[0].reshape(1, d)
    n2 = norm2_w[0].reshape(1, d)
    hnw = hg_norm_w[0].reshape(1, HEAD_DIM)
    fw = final_norm_w.reshape(1, d)

    xp = x_prompt.reshape(np_, d)
    xs = x_sample.reshape(ns, d)
    proj_p = _inproj(xp, n1, w_in_b, 256, w_in_b.shape[1] // 2)
    proj_s = _inproj(xs, n1, w_in_b, ns, w_in_b.shape[1] // 2)

    ohg_p, st_p = _hgrn(proj_p, hg_lb_logits, hnw, None, bp, tp, heads, 0, 4, 512)
    ohg_s, st_s = _hgrn(proj_s, hg_lb_logits, hnw, state_hgrn[0], bs, ts, heads, 0, heads, ts)
    osb_p, k_p, v_p = _sb_prompt(proj_p, bp, tp, heads, 4 * heads, 256, 2)
    osb_s = _sb_decode(proj_s, cache_sb_k[0], cache_sb_v[0], bs, ts, heads, 4 * heads, 512, 1024)

    rw = jnp.concatenate([router_group_w[0], router_expert_w[0]], axis=1)
    rw = jnp.pad(rw, ((0, 0), (0, LANES - rw.shape[1])))
    rwh = rw.astype(BF16)
    rw2 = jnp.stack([rwh, (rw - rwh.astype(F32)).astype(BF16)])
    rb = jnp.concatenate([router_group_b[0], router_expert_b[0]])
    rb = jnp.pad(rb, (0, LANES - rb.shape[0])).reshape(1, LANES)

    wa, wb = w_out_b[:hw], w_out_b[hw:]
    outs = _outproj(xp, ohg_p, osb_p, wa, wb, n2, rw2, rb, tm_tok, n, 0, None)
    x2, xn2, r = _outproj(xs, ohg_s, osb_s, wa, wb, n2, rw2, rb, tm_tok, n, np_, outs)

    pos, item_tile, item_expert, item_lo, item_hi, n_items, item_next, item_group = _dispatch_tables(r, tm_moe)
    pos = pos.reshape(n // tm_tok, 1, 2 * tm_tok)
    xg = _scatter(pos, xn2, tm_tok)
    ys = _experts(item_tile, item_expert, item_lo, item_hi, n_items, item_next, item_group, xg,
                  expert_w_gate[0], expert_w_up[0], expert_w_down[0], tm_moe)
    y_p = _combine(pos, x2, r, fw, ys, tm_tok, 0, np_)
    y_s = _combine(pos, x2, r, fw, ys, tm_tok, np_, ns)

    heads_s = lambda a: a.reshape(bs, ts, heads, HEAD_DIM).transpose(0, 2, 1, 3)[None]
    k_s = heads_s(proj_s[:, 5 * hw:6 * hw])
    v_s = heads_s(proj_s[:, 6 * hw:7 * hw])
    return (y_p.reshape(bp, tp, d), y_s.reshape(bs, ts, d), k_p[None], v_p[None], st_p[None],
            k_s, v_s, st_s[None])
```

```python
import functools

import jax
import jax.numpy as jnp
import numpy as np
from jax import lax
from jax.experimental import pallas as pl
from jax.experimental.pallas import tpu as pltpu

F32 = jnp.float32
BF16 = jnp.bfloat16
NORM_EPS = 1e-6
LANES = 128
HEAD_DIM = 128
HGRN_CHUNK = 64
HGRN_SUB = 16
N_GROUPS = 4
EXPERTS_PER_GROUP = 8
N_EXPERTS = N_GROUPS * EXPERTS_PER_GROUP
VMEM_LIMIT = 48 * 1024 * 1024
NEG_BIG = -1e30
LOG2E = 1.4426950408889634

_NT = (((1,), (1,)), ((), ()))
_TN = (((0,), (0,)), ((), ()))


def _params(sem):
    return pltpu.CompilerParams(dimension_semantics=sem, vmem_limit_bytes=VMEM_LIMIT)


def _sigmoid(x):
    return 1.0 / (1.0 + jnp.exp(-x))


def _split3(x):
    h = x.astype(BF16)
    r = x - h.astype(F32)
    m = r.astype(BF16)
    l = (r - m.astype(F32)).astype(BF16)
    return h, m, l


def _cat(parts, axis):
    return parts[0] if len(parts) == 1 else jnp.concatenate(parts, axis=axis)


def _inproj_kernel(x_ref, nw_ref, w_ref, o_ref):
    x = x_ref[...]
    ms = jnp.mean(x * x, axis=-1, keepdims=True)
    xn = ((x * lax.rsqrt(ms + NORM_EPS)) * nw_ref[...]).astype(BF16)
    o_ref[...] = jnp.dot(xn, w_ref[...], preferred_element_type=F32)


def _inproj(x, nw, w, tm, tn):
    n, d = x.shape
    c = w.shape[1]
    return pl.pallas_call(
        _inproj_kernel,
        grid=(c // tn, n // tm),
        in_specs=[pl.BlockSpec((tm, d), lambda j, i: (i, 0)),
                  pl.BlockSpec((1, d), lambda j, i: (0, 0)),
                  pl.BlockSpec((d, tn), lambda j, i: (0, j))],
        out_specs=pl.BlockSpec((tm, tn), lambda j, i: (i, j)),
        out_shape=jax.ShapeDtypeStruct((n, c), F32),
        compiler_params=_params(("arbitrary", "arbitrary")),
        name="inproj",
    )(x, nw, w)


def _hgrn_kernel(*refs, tb, chunk, sub, hp, has_state):
    if has_state:
        (hq_ref, hf_ref, hi_ref, hg_ref, lbl_ref, nw_ref, tri_ref, wsel_ref, s0_ref,
         o_ref, sout_ref, st_ref, q_scr, b_scr, c_scr) = refs
    else:
        (hq_ref, hf_ref, hi_ref, hg_ref, lbl_ref, nw_ref, tri_ref, wsel_ref,
         o_ref, sout_ref, st_ref, q_scr, b_scr, c_scr) = refs
    n_sub = chunk // sub
    ti = pl.program_id(2)
    hs = [slice(h * HEAD_DIM, (h + 1) * HEAD_DIM) for h in range(hp)]

    lg = lbl_ref[...]
    lg = jnp.exp(lg - jnp.max(lg, axis=0, keepdims=True))
    lb = lg[0:1, :] / jnp.sum(lg, axis=0, keepdims=True)

    @pl.when(ti == 0)
    def _():
        for h in range(hp):
            st_ref[h] = s0_ref[0, h].T if has_state else jnp.zeros((HEAD_DIM, HEAD_DIM), F32)

    row = lax.broadcasted_iota(jnp.int32, (chunk, chunk), 0)
    col = lax.broadcasted_iota(jnp.int32, (chunk, chunk), 1)
    blk_diff = row // sub - col // sub
    diag_mask = (blk_diff == 0) & (col <= row)
    diag_mask_all = _cat([diag_mask] * hp, 0)
    tri = tri_ref[...]
    wsel = wsel_ref[...]
    nw = nw_ref[...]

    def prepare(ci, slot):
        r0 = pl.multiple_of(ci * chunk, chunk)
        hq = hq_ref[pl.ds(r0, chunk), :]
        hf = hf_ref[pl.ds(r0, chunk), :]
        f = lb + (1.0 - lb) * _sigmoid(hf)
        k = 1.0 - f
        l1, l2, l3 = _split3(jnp.log(f) * LOG2E)
        b = (jnp.dot(tri, l1, preferred_element_type=F32) + jnp.dot(tri, l2, preferred_element_type=F32)
             + jnp.dot(tri, l3, preferred_element_type=F32))
        q_scr[slot] = hq * _sigmoid(hq)
        b_scr[slot] = b
        c_scr[slot] = b - jnp.log(k) * LOG2E

    def main(ci, slot):
        r0 = pl.multiple_of(ci * chunk, chunk)
        v = hi_ref[pl.ds(r0, chunk), :]
        hg = hg_ref[pl.ds(r0, chunk), :]
        q = q_scr[slot]
        b = b_scr[slot]
        c = c_scr[slot]
        b_rows = b_scr.at[slot]
        c_rows = c_scr.at[slot]
        vb = v.astype(BF16)
        b_last = b_rows[pl.ds(chunk - 1, 1), :]
        qd = (q * jnp.exp2(b)).astype(BF16)
        kd = jnp.exp2(b_last - c).astype(BF16)
        s_decay = jnp.exp2(b_last)

        slabs = []
        for r in range(sub):
            parts = []
            for i in range(n_sub):
                c_row = c_rows[pl.ds(i * sub + r, 1), :]
                sl = slice(i * sub, (i + 1) * sub)
                parts.append(q[sl] * jnp.exp2(jnp.minimum(b[sl] - c_row, 0.0)))
            slabs.append(_cat(parts, 0).astype(BF16))
        xcat = _cat([_cat([s[:, hs[h]] for s in slabs], 1) for h in range(hp)], 0)
        a_all = jnp.where(diag_mask_all, jnp.dot(xcat, wsel, preferred_element_type=F32), 0.0)

        if n_sub > 1:
            starts = [None] + [b_rows[pl.ds(i * sub - 1, 1), :] for i in range(1, n_sub)]
            ends = [b_rows[pl.ds(i * sub + sub - 1, 1), :] for i in range(n_sub)]
            qh, kh = [], []
            for i in range(n_sub):
                sl = slice(i * sub, (i + 1) * sub)
                qh.append(q[sl] if i == 0 else q[sl] * jnp.exp2(b[sl] - starts[i]))
                kh.append(jnp.exp2(ends[i] - c[sl]))
            khat = _cat(kh, 0).astype(BF16)
            qds = []
            for d in range(1, n_sub):
                parts = []
                for i in range(n_sub):
                    if i < d:
                        parts.append(jnp.zeros((sub, hp * HEAD_DIM), F32))
                    elif d == 1:
                        parts.append(qh[i])
                    else:
                        parts.append(qh[i] * jnp.exp2(starts[i] - ends[i - d]))
                qds.append(_cat(parts, 0).astype(BF16))
            qstack = _cat(qds, 0)

        outs = []
        for h in range(hp):
            st = st_ref[h]
            o = lax.dot_general(qd[:, hs[h]], st.astype(BF16), _NT, preferred_element_type=F32)
            st_ref[h] = st * s_decay[:, hs[h]] + lax.dot_general(vb[:, hs[h]], kd[:, hs[h]], _TN,
                                                                 preferred_element_type=F32)
            a = a_all[h * chunk:(h + 1) * chunk]
            if n_sub > 1:
                m = lax.dot_general(qstack[:, hs[h]], khat[:, hs[h]], _NT, preferred_element_type=F32)
                for d in range(1, n_sub):
                    a = a + jnp.where(blk_diff == d, m[(d - 1) * chunk:d * chunk], 0.0)
            o = o + jnp.dot(a.astype(BF16), vb[:, hs[h]], preferred_element_type=F32)
            ms = jnp.mean(o * o, axis=-1, keepdims=True)
            outs.append((o * lax.rsqrt(ms + NORM_EPS)) * nw[:, hs[h]])
        g = hg * _sigmoid(hg)
        o_ref[pl.ds(r0, chunk), :] = (_cat(outs, 1) * g).astype(o_ref.dtype)

    n_chunks = tb // chunk
    prepare(0, 0)
    if n_chunks == 1:
        main(0, 0)
    else:
        def pair(pp, carry):
            ci = 2 * pp
            prepare(ci + 1, 1)
            main(ci, 0)
            prepare(jnp.minimum(ci + 2, n_chunks - 1), 0)
            main(ci + 1, 1)
            return carry

        lax.fori_loop(0, n_chunks // 2, pair, 0)

    @pl.when(ti == pl.num_programs(2) - 1)
    def _():
        for h in range(hp):
            sout_ref[0, h] = st_ref[h].T


def _hgrn_consts(chunk, sub):
    t = np.arange(chunk)
    tri = (t[None, :] <= t[:, None]).astype(np.float32)
    rows = np.arange(sub * HEAD_DIM) // HEAD_DIM
    wsel = (rows[:, None] == (t[None, :] % sub)).astype(np.float32)
    return jnp.asarray(tri, BF16), jnp.asarray(wsel, BF16)


def _hgrn(proj, lb_logits, norm_w, state, batch, seq, heads, col0, hp, tb):
    chunk = min(HGRN_CHUNK, seq)
    sub = min(HGRN_SUB, chunk)
    tri, wsel = _hgrn_consts(chunk, sub)
    has_state = state is not None
    nt = seq // tb
    w = hp * HEAD_DIM

    def col_spec(g):
        return pl.BlockSpec((tb, w), lambda b, h, t: (b * nt + t, (col0 + g * heads) // hp + h))

    const = lambda b, h, t: (0, 0)
    in_specs = [col_spec(0), col_spec(1), col_spec(2), col_spec(3),
                pl.BlockSpec((lb_logits.shape[0], w), lambda b, h, t: (0, h)),
                pl.BlockSpec((1, w), const),
                pl.BlockSpec(tri.shape, const),
                pl.BlockSpec(wsel.shape, const)]
    args = [proj, proj, proj, proj, lb_logits, jnp.tile(norm_w, (1, hp)), tri, wsel]
    if has_state:
        in_specs.append(pl.BlockSpec((1, hp, HEAD_DIM, HEAD_DIM), lambda b, h, t: (b, h, 0, 0)))
        args.append(state)
    return pl.pallas_call(
        functools.partial(_hgrn_kernel, tb=tb, chunk=chunk, sub=sub, hp=hp, has_state=has_state),
        grid=(batch, heads // hp, nt),
        in_specs=in_specs,
        out_specs=[pl.BlockSpec((tb, w), lambda b, h, t: (b * nt + t, h)),
                   pl.BlockSpec((1, hp, HEAD_DIM, HEAD_DIM), lambda b, h, t: (b, h, 0, 0))],
        out_shape=[jax.ShapeDtypeStruct((batch * seq, heads * HEAD_DIM), BF16),
                   jax.ShapeDtypeStruct((batch, heads, HEAD_DIM, HEAD_DIM), F32)],
        scratch_shapes=[pltpu.VMEM((hp, HEAD_DIM, HEAD_DIM), F32),
                        pltpu.VMEM((2, chunk, w), F32),
                        pltpu.VMEM((2, chunk, w), F32),
                        pltpu.VMEM((2, chunk, w), F32)],
        compiler_params=_params(("arbitrary", "arbitrary", "arbitrary")),
        name="hgrn_state" if has_state else "hgrn",
    )(*args)


def _sb_blocks(qs, ks, vs, u, c, mask):
    tq = qs[0].shape[0]
    z = _cat([lax.dot_general(q, k, _NT, preferred_element_type=F32) for q, k in zip(qs, ks)], 0)
    nz = -z
    sp = jnp.maximum(nz, 0.0) + jnp.log2(1.0 + jnp.exp2(jnp.minimum(z, nz)))
    l = nz - sp
    if mask is not None:
        l = jnp.where(mask, l, 0.0)
    s = jnp.dot(l.astype(BF16), u, preferred_element_type=F32)
    p = jnp.exp2(s + c - sp)
    if mask is not None:
        p = jnp.where(mask, p, 0.0)
    p = p.astype(BF16)
    pv = _cat([jnp.dot(p[h * tq:(h + 1) * tq], vs[h], preferred_element_type=F32) for h in range(len(qs))], 0)
    return pv, c + (s[:, 0:1] + l[:, 0:1])


def _sb_suffix(tk):
    s = np.arange(tk)
    u = (s[:, None] > s[None, :]).astype(np.float32)
    return jnp.asarray(u, BF16)


def _sb_prompt_kernel(q_ref, k_ref, v_ref, u_ref, o_ref, ko_ref, vo_ref, acc_ref, c_ref, kb_ref, vb_ref,
                      lc_ref, sp_ref, l0_ref, *, tq, hp, scale):
    i = pl.program_id(2)
    hs = [slice(h * HEAD_DIM, (h + 1) * HEAD_DIM) for h in range(hp)]

    for h in range(hp):
        ko_ref[0, h] = k_ref[:, hs[h]]
        vo_ref[0, h] = v_ref[:, hs[h]]
    own = pl.ds(pl.multiple_of(i * tq, tq), tq)
    kb_ref[own, :] = k_ref[...].astype(BF16)
    vb_ref[own, :] = v_ref[...].astype(BF16)

    q = (q_ref[...] * scale).astype(BF16)
    usuf = u_ref[...]

    def scores(j, slot, diagonal):
        kb = kb_ref[pl.ds(pl.multiple_of(j * tq, tq), tq), :]
        z = _cat([lax.dot_general(q[:, hs[h]], kb[:, hs[h]], _NT, preferred_element_type=F32) for h in range(hp)], 0)
        nz = -z
        sp = jnp.maximum(nz, 0.0) + jnp.log2(1.0 + jnp.exp2(jnp.minimum(z, nz)))
        l = nz - sp
        if diagonal:
            row = lax.broadcasted_iota(jnp.int32, (tq, tq), 0)
            col = lax.broadcasted_iota(jnp.int32, (tq, tq), 1)
            mask = _cat([col < row] * hp, 0)
            l = jnp.where(mask, l, 0.0)
            sp = jnp.where(mask, sp, -NEG_BIG)
        lc_ref[slot] = l.astype(BF16)
        sp_ref[slot] = sp
        l0_ref[slot] = l[:, 0:1]

    def values(j, slot):
        vb = vb_ref[pl.ds(pl.multiple_of(j * tq, tq), tq), :]
        s = jnp.dot(lc_ref[slot], usuf, preferred_element_type=F32)
        c = c_ref[...]
        p = jnp.exp2(s + c - sp_ref[slot]).astype(BF16)
        acc_ref[...] += _cat([jnp.dot(p[h * tq:(h + 1) * tq], vb[:, hs[h]], preferred_element_type=F32)
                              for h in range(hp)], 0)
        c_ref[...] = c + (s[:, 0:1] + l0_ref[slot])

    acc_ref[...] = jnp.zeros_like(acc_ref)
    c_ref[...] = jnp.zeros_like(c_ref)
    scores(i, 0, True)

    def pair(pp, carry):
        j = i - 1 - 2 * pp
        scores(j, 1, False)
        values(j + 1, 0)
        scores(j - 1, 0, False)
        values(j, 1)
        return carry

    lax.fori_loop(0, i // 2, pair, 0)

    @pl.when(i % 2 == 1)
    def _():
        scores(0, 1, False)
        values(1, 0)
        values(0, 1)

    @pl.when(i % 2 == 0)
    def _():
        values(0, 0)
    o_ref[...] = _cat([acc_ref[pl.ds(h * tq, tq), :] for h in range(hp)], 1).astype(o_ref.dtype)


def _sb_prompt(proj, batch, seq, heads, col0, tq, hp):
    nq = seq // tq
    w = hp * HEAD_DIM
    usuf = _sb_suffix(tq)
    kv_shape = jax.ShapeDtypeStruct((batch, heads, seq, HEAD_DIM), F32)
    return pl.pallas_call(
        functools.partial(_sb_prompt_kernel, tq=tq, hp=hp, scale=HEAD_DIM ** -0.5 * LOG2E),
        grid=(batch, heads // hp, nq),
        in_specs=[pl.BlockSpec((tq, w), lambda b, h, i: (b * nq + i, col0 // hp + h)),
                  pl.BlockSpec((tq, w), lambda b, h, i: (b * nq + i, (col0 + heads) // hp + h)),
                  pl.BlockSpec((tq, w), lambda b, h, i: (b * nq + i, (col0 + 2 * heads) // hp + h)),
                  pl.BlockSpec(usuf.shape, lambda b, h, i: (0, 0))],
        out_specs=[pl.BlockSpec((tq, w), lambda b, h, i: (b * nq + i, h)),
                   pl.BlockSpec((1, hp, tq, HEAD_DIM), lambda b, h, i: (b, h, i, 0)),
                   pl.BlockSpec((1, hp, tq, HEAD_DIM), lambda b, h, i: (b, h, i, 0))],
        out_shape=[jax.ShapeDtypeStruct((batch * seq, heads * HEAD_DIM), BF16), kv_shape, kv_shape],
        scratch_shapes=[pltpu.VMEM((hp * tq, HEAD_DIM), F32), pltpu.VMEM((hp * tq, 1), F32),
                        pltpu.VMEM((seq, w), BF16), pltpu.VMEM((seq, w), BF16),
                        pltpu.VMEM((2, hp * tq, tq), BF16), pltpu.VMEM((2, hp * tq, tq), F32),
                        pltpu.VMEM((2, hp * tq, 1), F32)],
        compiler_params=_params(("arbitrary", "arbitrary", "arbitrary")),
        name="sb_prompt",
    )(proj, proj, proj, usuf)


def _sb_decode_kernel(q_ref, k_ref, v_ref, pk_ref, pv_ref, un_ref, up_ref, o_ref, acc_ref, c_ref,
                      *, seq, heads, tk, scale):
    j = pl.program_id(1)
    hs = [slice(h * HEAD_DIM, (h + 1) * HEAD_DIM) for h in range(heads)]
    q = (q_ref[...] * scale).astype(BF16)
    qs = [q[:, hs[h]] for h in range(heads)]

    @pl.when(j == 0)
    def _():
        pad = jnp.zeros((LANES - seq, heads * HEAD_DIM), F32)
        kn = jnp.concatenate([k_ref[...], pad], axis=0).astype(BF16)
        vn = jnp.concatenate([v_ref[...], pad], axis=0).astype(BF16)
        row = lax.broadcasted_iota(jnp.int32, (seq, LANES), 0)
        col = lax.broadcasted_iota(jnp.int32, (seq, LANES), 1)
        mask = _cat([col < row] * heads, 0)
        pv, c = _sb_blocks(qs, [kn[:, hs[h]] for h in range(heads)], [vn[:, hs[h]] for h in range(heads)],
                           un_ref[...], jnp.zeros((heads * seq, 1), F32), mask)
        acc_ref[...] = pv
        c_ref[...] = c

    usuf = up_ref[...]
    for sblk in reversed(range(pk_ref.shape[2] // tk)):
        ks = [pk_ref[0, h, pl.ds(sblk * tk, tk), :].astype(BF16) for h in range(heads)]
        vs = [pv_ref[0, h, pl.ds(sblk * tk, tk), :].astype(BF16) for h in range(heads)]
        pv, c = _sb_blocks(qs, ks, vs, usuf, c_ref[...], None)
        acc_ref[...] += pv
        c_ref[...] = c

    @pl.when(j == pl.num_programs(1) - 1)
    def _():
        o_ref[...] = _cat([acc_ref[pl.ds(h * seq, seq), :] for h in range(heads)], 1).astype(o_ref.dtype)


def _sb_decode(proj, past_k, past_v, batch, seq, heads, col0, tk, tkb):
    past = past_k.shape[2]
    nblk = past // tkb
    w = heads * HEAD_DIM
    un, up = _sb_suffix(LANES), _sb_suffix(tk)
    kv_spec = pl.BlockSpec((1, heads, tkb, HEAD_DIM), lambda b, j: (b, 0, nblk - 1 - j, 0))
    return pl.pallas_call(
        functools.partial(_sb_decode_kernel, seq=seq, heads=heads, tk=tk, scale=HEAD_DIM ** -0.5 * LOG2E),
        grid=(batch, nblk),
        in_specs=[pl.BlockSpec((seq, w), lambda b, j: (b, col0 // heads)),
                  pl.BlockSpec((seq, w), lambda b, j: (b, col0 // heads + 1)),
                  pl.BlockSpec((seq, w), lambda b, j: (b, col0 // heads + 2)),
                  kv_spec, kv_spec,
                  pl.BlockSpec(un.shape, lambda b, j: (0, 0)),
                  pl.BlockSpec(up.shape, lambda b, j: (0, 0))],
        out_specs=pl.BlockSpec((seq, w), lambda b, j: (b, 0)),
        out_shape=jax.ShapeDtypeStruct((batch * seq, w), BF16),
        scratch_shapes=[pltpu.VMEM((heads * seq, HEAD_DIM), F32), pltpu.VMEM((heads * seq, 1), F32)],
        compiler_params=_params(("arbitrary", "arbitrary")),
        name="sb_decode",
    )(proj, proj, proj, past_k, past_v, un, up)


def _route(logits):
    lane = lax.broadcasted_iota(jnp.int32, logits.shape, 1).astype(F32)
    first = lambda hit: jnp.min(jnp.where(hit, lane, float(LANES)), axis=-1, keepdims=True)
    is_g = lane < N_GROUPS
    gl = jnp.where(is_g, logits, NEG_BIG)
    gmax = jnp.max(gl, axis=-1, keepdims=True)
    gidx = first(gl == gmax)
    gsum = jnp.sum(jnp.where(is_g, jnp.exp(gl - gmax), 0.0), axis=-1, keepdims=True)
    gw = 1.0 / gsum
    lo = N_GROUPS + EXPERTS_PER_GROUP * gidx
    in_g = (lane >= lo) & (lane < lo + EXPERTS_PER_GROUP)
    el = jnp.where(in_g, logits, NEG_BIG)
    m1 = jnp.max(el, axis=-1, keepdims=True)
    i1 = first(in_g & (el == m1))
    rest = in_g & (lane != i1)
    el2 = jnp.where(rest, logits, NEG_BIG)
    m2 = jnp.max(el2, axis=-1, keepdims=True)
    i2 = first(rest & (el2 == m2))
    t = jnp.exp(m2 - m1)
    w0 = gw / (1.0 + t)
    w1 = gw * t / (1.0 + t)
    e0 = i1 - N_GROUPS
    e1 = i2 - N_GROUPS
    return jnp.where(lane == 0, e0, jnp.where(lane == 1, e1, jnp.where(lane == 2, w0, jnp.where(lane == 3, w1, 0.0))))


def _outproj_kernel(x_ref, a_ref, b_ref, wa_ref, wb_ref, nw_ref, rw_ref, rb_ref, *rest):
    x2_ref, xn_ref, r_ref = rest[-3:]
    acc = jnp.dot(a_ref[...], wa_ref[...], preferred_element_type=F32)
    acc = acc + jnp.dot(b_ref[...], wb_ref[...], preferred_element_type=F32)
    x2 = x_ref[...] + acc
    x2_ref[...] = x2
    ms = jnp.mean(x2 * x2, axis=-1, keepdims=True)
    xn = (x2 * lax.rsqrt(ms + NORM_EPS)) * nw_ref[...]
    xn_ref[...] = xn
    xh = xn.astype(BF16)
    xl = (xn - xh.astype(F32)).astype(BF16)
    wh = rw_ref[0]
    logits = (jnp.dot(xh, wh, preferred_element_type=F32) + jnp.dot(xh, rw_ref[1], preferred_element_type=F32)
              + jnp.dot(xl, wh, preferred_element_type=F32)) + rb_ref[...]
    r_ref[...] = _route(logits)


def _outproj(x, oa, ob, wa, wb, nw, rw, rb, tm, n_total, row0, prev):
    n, d = x.shape
    blk0 = row0 // tm
    row = lambda i: (i, 0)
    orow = lambda i: (blk0 + i, 0)
    const = lambda i: (0, 0)
    in_specs = [pl.BlockSpec((tm, d), row),
                pl.BlockSpec((tm, oa.shape[1]), row),
                pl.BlockSpec((tm, ob.shape[1]), row),
                pl.BlockSpec(wa.shape, const),
                pl.BlockSpec(wb.shape, const),
                pl.BlockSpec((1, d), const),
                pl.BlockSpec(rw.shape, lambda i: (0, 0, 0)),
                pl.BlockSpec((1, LANES), const)]
    args = [x, oa, ob, wa, wb, nw, rw, rb]
    aliases = {}
    if prev is not None:
        in_specs += [pl.BlockSpec(memory_space=pl.ANY)] * 3
        aliases = {len(args) + t: t for t in range(3)}
        args += list(prev)
    return pl.pallas_call(
        _outproj_kernel,
        grid=(n // tm,),
        in_specs=in_specs,
        out_specs=[pl.BlockSpec((tm, d), orow), pl.BlockSpec((tm, d), orow), pl.BlockSpec((tm, LANES), orow)],
        out_shape=[jax.ShapeDtypeStruct((n_total, d), F32), jax.ShapeDtypeStruct((n_total, d), F32),
                   jax.ShapeDtypeStruct((n_total, LANES), F32)],
        input_output_aliases=aliases,
        compiler_params=_params(("arbitrary",)),
        name="outproj",
    )(*args)


def _row_out(x_ref, r, dst_hbm, row, sem):
    return pltpu.make_async_copy(x_ref.at[pl.ds(r, 1), :], dst_hbm.at[pl.ds(row, 1), :], sem)


def _scatter_kernel(pos_ref, x_ref, xs_hbm, sem, *, tm):
    def issue(r, carry):
        _row_out(x_ref, r, xs_hbm, pos_ref[0, 0, 2 * r], sem).start()
        _row_out(x_ref, r, xs_hbm, pos_ref[0, 0, 2 * r + 1], sem).start()
        return carry
    lax.fori_loop(0, tm, issue, 0, unroll=8)

    def drain(r, carry):
        _row_out(x_ref, r, xs_hbm, 0, sem).wait()
        _row_out(x_ref, r, xs_hbm, 0, sem).wait()
        return carry
    lax.fori_loop(0, tm, drain, 0, unroll=8)


def _scatter(pos, x, tm):
    n, d = x.shape
    return pl.pallas_call(
        functools.partial(_scatter_kernel, tm=tm),
        grid=(n // tm,),
        in_specs=[pl.BlockSpec((1, 1, 2 * tm), lambda m: (m, 0, 0), memory_space=pltpu.SMEM),
                  pl.BlockSpec((tm, d), lambda m: (m, 0))],
        out_specs=pl.BlockSpec(memory_space=pl.ANY),
        out_shape=jax.ShapeDtypeStruct((2 * n, d), x.dtype),
        scratch_shapes=[pltpu.SemaphoreType.DMA(())],
        compiler_params=_params(("arbitrary",)),
        name="scatter",
    )(pos, x)


def _weight_copies(e, slot, w_hbm, w_stage, sem):
    return [pltpu.make_async_copy(w_hbm[t].at[e], w_stage[t].at[slot], sem.at[slot, t]) for t in range(3)]


def _experts_kernel(it_ref, ie_ref, lo_ref, hi_ref, ni_ref, nx_ref, ig_ref, x_ref, wg_hbm, wu_hbm, wd_hbm, o_ref,
                    sg, su, sd, g_scr, u_scr, d_scr, sem):
    w = pl.program_id(0)
    w_hbm, w_stage = (wg_hbm, wu_hbm, wd_hbm), (sg, su, sd)

    @pl.when(w < ni_ref[0])
    def _():
        prev = jnp.maximum(w - 1, 0)
        e = ie_ref[w]
        new_expert = (w == 0) | (e != ie_ref[prev])
        new_tile = (w == 0) | (it_ref[w] != it_ref[prev])
        slot = ig_ref[w] % 2

        @pl.when(w == 0)
        def _():
            for cp in _weight_copies(e, slot, w_hbm, w_stage, sem):
                cp.start()

        @pl.when(new_expert)
        def _():
            for cp in _weight_copies(e, slot, w_hbm, w_stage, sem):
                cp.wait()

            @pl.when(nx_ref[w] != e)
            def _():
                for cp in _weight_copies(nx_ref[w], 1 - slot, w_hbm, w_stage, sem):
                    cp.start()

            g_scr[...] = sg[slot].astype(BF16)
            u_scr[...] = su[slot].astype(BF16)
            d_scr[...] = sd[slot].astype(BF16)

        x = x_ref[...].astype(BF16)
        g = jnp.dot(x, g_scr[...], preferred_element_type=F32)
        u = jnp.dot(x, u_scr[...], preferred_element_type=F32)
        rows = lax.broadcasted_iota(jnp.int32, g.shape, 0)
        mine = (rows >= lo_ref[w]) & (rows < hi_ref[w])
        hid = jnp.where(mine, (g * _sigmoid(g)) * u, 0.0).astype(BF16)
        y = jnp.dot(hid, d_scr[...], preferred_element_type=F32)

        @pl.when(new_tile)
        def _():
            o_ref[...] = y

        @pl.when(jnp.logical_not(new_tile))
        def _():
            o_ref[...] += y


def _experts(item_tile, item_expert, item_lo, item_hi, n_items, item_next, item_group, xs, wg, wu, wd, tm):
    rows, d = xs.shape
    ff = wg.shape[2]
    items = item_tile.shape[0]
    tile = lambda w, it, *_: (it[w], 0)
    hbm = pl.BlockSpec(memory_space=pl.ANY)
    return pl.pallas_call(
        _experts_kernel,
        grid_spec=pltpu.PrefetchScalarGridSpec(
            num_scalar_prefetch=7,
            grid=(items,),
            in_specs=[pl.BlockSpec((tm, d), tile), hbm, hbm, hbm],
            out_specs=pl.BlockSpec((tm, d), tile),
            scratch_shapes=[pltpu.VMEM((2, d, ff), F32), pltpu.VMEM((2, d, ff), F32), pltpu.VMEM((2, ff, d), F32),
                            pltpu.VMEM((d, ff), BF16), pltpu.VMEM((d, ff), BF16), pltpu.VMEM((ff, d), BF16),
                            pltpu.SemaphoreType.DMA((2, 3))]),
        out_shape=jax.ShapeDtypeStruct((rows, d), F32),
        compiler_params=_params(("arbitrary",)),
        name="experts",
    )(item_tile, item_expert, item_lo, item_hi, n_items, item_next, item_group, xs, wg, wu, wd)


def _row_in(src_hbm, row, dst, r, sem):
    return pltpu.make_async_copy(src_hbm.at[pl.ds(row, 1), :], dst.at[pl.ds(r, 1), :], sem)


def _combine_kernel(pos_ref, nxt_ref, x_ref, r_ref, fw_ref, ys_hbm, o_ref, buf, sem, *, tm, ntiles):
    m = pl.program_id(0)
    slot = m % 2

    def issue(tbl, s):
        def go(r, carry):
            _row_in(ys_hbm, tbl[0, 0, 2 * r], buf.at[s, 0], r, sem.at[s]).start()
            _row_in(ys_hbm, tbl[0, 0, 2 * r + 1], buf.at[s, 1], r, sem.at[s]).start()
            return carry
        lax.fori_loop(0, tm, go, 0, unroll=8)

    @pl.when(m == 0)
    def _():
        issue(pos_ref, 0)

    @pl.when(m + 1 < ntiles)
    def _():
        issue(nxt_ref, 1 - slot)

    def drain(r, carry):
        _row_in(ys_hbm, 0, buf.at[slot, 0], r, sem.at[slot]).wait()
        _row_in(ys_hbm, 0, buf.at[slot, 1], r, sem.at[slot]).wait()
        return carry
    lax.fori_loop(0, tm, drain, 0, unroll=8)
    rt = r_ref[...]
    x = x_ref[...] + (buf[slot, 0] * rt[:, 2:3] + buf[slot, 1] * rt[:, 3:4])
    ms = jnp.mean(x * x, axis=-1, keepdims=True)
    o_ref[...] = (x * lax.rsqrt(ms + NORM_EPS)) * fw_ref[...]


def _combine(pos, x2, r, fw, ys, tm, row0, n):
    d = x2.shape[1]
    blk0 = row0 // tm
    ntiles = n // tm
    smem = functools.partial(pl.BlockSpec, memory_space=pltpu.SMEM)
    return pl.pallas_call(
        functools.partial(_combine_kernel, tm=tm, ntiles=ntiles),
        grid=(ntiles,),
        in_specs=[smem((1, 1, 2 * tm), lambda m: (blk0 + m, 0, 0)),
                  smem((1, 1, 2 * tm), lambda m: (blk0 + jnp.minimum(m + 1, ntiles - 1), 0, 0)),
                  pl.BlockSpec((tm, d), lambda m: (blk0 + m, 0)),
                  pl.BlockSpec((tm, LANES), lambda m: (blk0 + m, 0)),
                  pl.BlockSpec((1, d), lambda m: (0, 0)),
                  pl.BlockSpec(memory_space=pl.ANY)],
        out_specs=pl.BlockSpec((tm, d), lambda m: (m, 0)),
        out_shape=jax.ShapeDtypeStruct((n, d), F32),
        scratch_shapes=[pltpu.VMEM((2, 2, tm, d), F32), pltpu.SemaphoreType.DMA((2,))],
        compiler_params=_params(("arbitrary",)),
        name="combine",
    )(pos, pos, x2, r, fw, ys)


def _dispatch_tables(r, tm):
    n = r.shape[0]
    ntiles = (2 * n) // tm
    items = ntiles + N_EXPERTS - 1
    ids = jnp.arange(N_EXPERTS, dtype=jnp.int32)
    e = r[:, 0:2].astype(jnp.int32).reshape(-1)
    onehot = (e[:, None] == ids[None, :]).astype(jnp.int32)
    incl = jnp.cumsum(onehot, axis=0)
    counts = incl[-1]
    cend = jnp.cumsum(counts)
    cstart = cend - counts
    pos = jnp.sum(onehot * (incl - 1 + cstart[None, :]), axis=1)
    first_tile = cstart // tm
    n_it = jnp.where(counts > 0, (cend - 1) // tm - first_tile + 1, 0)
    it_end = jnp.cumsum(n_it)
    it_start = it_end - n_it
    n_items = it_end[-1]
    w = jnp.minimum(jnp.arange(items, dtype=jnp.int32), n_items - 1)
    ie = jnp.sum((it_end[None, :] <= w[:, None]).astype(jnp.int32), axis=1)
    sel = (ie[:, None] == ids[None, :]).astype(jnp.int32)
    pick = lambda tbl: jnp.sum(sel * tbl[None, :], axis=1)
    itile = pick(first_tile) + (w - pick(it_start))
    lo = jnp.maximum(pick(cstart) - itile * tm, 0)
    hi = jnp.minimum(pick(cend) - itile * tm, tm)
    used = counts > 0
    later = used[None, :] & (ids[None, :] > ids[:, None])
    next_used = jnp.min(jnp.where(later, ids[None, :], N_EXPERTS), axis=1)
    next_used = jnp.where(next_used == N_EXPERTS, ids, next_used)
    group = jnp.cumsum(used.astype(jnp.int32)) - 1
    i32 = lambda a: a.astype(jnp.int32)
    return (i32(pos), i32(itile), i32(ie), i32(lo), i32(hi), i32(n_items).reshape(1), i32(pick(next_used)),
            i32(pick(group)))


def kernel(x_prompt, x_sample, cache_sb_k, cache_sb_v, state_hgrn, w_in, hg_lb_logits, hg_norm_w, w_out, norm1_w,
           norm2_w, router_group_w, router_group_b, router_expert_w, router_expert_b, expert_w_gate, expert_w_up,
           expert_w_down, final_norm_w):
    bp, tp, d = x_prompt.shape
    bs, ts, _ = x_sample.shape
    np_, ns = bp * tp, bs * ts
    n = np_ + ns
    heads = w_in.shape[2] // (7 * HEAD_DIM)
    hw = heads * HEAD_DIM
    tm_tok = 256
    tm_moe = 256

    w_in_b = w_in[0].astype(BF16)
    w_out_b = w_out[0].astype(BF16)
    n1 = norm1_w[0].reshape(1, d)
    n2 = norm2_w[0].reshape(1, d)
    hnw = hg_norm_w[0].reshape(1, HEAD_DIM)
    fw = final_norm_w.reshape(1, d)

    xp = x_prompt.reshape(np_, d)
    xs = x_sample.reshape(ns, d)
    proj_p = _inproj(xp, n1, w_in_b, 256, w_in_b.shape[1] // 2)
    proj_s = _inproj(xs, n1, w_in_b, ns, w_in_b.shape[1] // 2)

    ohg_p, st_p = _hgrn(proj_p, hg_lb_logits, hnw, None, bp, tp, heads, 0, 4, 512)
    ohg_s, st_s = _hgrn(proj_s, hg_lb_logits, hnw, state_hgrn[0], bs, ts, heads, 0, heads, ts)
    osb_p, k_p, v_p = _sb_prompt(proj_p, bp, tp, heads, 4 * heads, 256, 4)
    osb_s = _sb_decode(proj_s, cache_sb_k[0], cache_sb_v[0], bs, ts, heads, 4 * heads, 512, 1024)

    rw = jnp.concatenate([router_group_w[0], router_expert_w[0]], axis=1)
    rw = jnp.pad(rw, ((0, 0), (0, LANES - rw.shape[1])))
    rwh = rw.astype(BF16)
    rw2 = jnp.stack([rwh, (rw - rwh.astype(F32)).astype(BF16)])
    rb = jnp.concatenate([router_group_b[0], router_expert_b[0]])
    rb = jnp.pad(rb, (0, LANES - rb.shape[0])).reshape(1, LANES)

    wa, wb = w_out_b[:hw], w_out_b[hw:]
    outs = _outproj(xp, ohg_p, osb_p, wa, wb, n2, rw2, rb, tm_tok, n, 0, None)
    x2, xn2, r = _outproj(xs, ohg_s, osb_s, wa, wb, n2, rw2, rb, tm_tok, n, np_, outs)

    pos, item_tile, item_expert, item_lo, item_hi, n_items, item_next, item_group = _dispatch_tables(r, tm_moe)
    pos = pos.reshape(n // tm_tok, 1, 2 * tm_tok)
    xg = _scatter(pos, xn2, tm_tok)
    ys = _experts(item_tile, item_expert, item_lo, item_hi, n_items, item_next, item_group, xg,
                  expert_w_gate[0], expert_w_up[0], expert_w_down[0], tm_moe)
    y_p = _combine(pos, x2, r, fw, ys, tm_tok, 0, np_)
    y_s = _combine(pos, x2, r, fw, ys, tm_tok, np_, ns)

    heads_s = lambda a: a.reshape(bs, ts, heads, HEAD_DIM).transpose(0, 2, 1, 3)[None]
    k_s = heads_s(proj_s[:, 5 * hw:6 * hw])
    v_s = heads_s(proj_s[:, 6 * hw:7 * hw])
    return (y_p.reshape(bp, tp, d), y_s.reshape(bs, ts, d), k_p[None], v_p[None], st_p[None],
            k_s, v_s, st_s[None])
```

```python
import functools

import jax
import jax.numpy as jnp
import numpy as np
from jax import lax
from jax.experimental import pallas as pl
from jax.experimental.pallas import tpu as pltpu

F32 = jnp.float32
BF16 = jnp.bfloat16
NORM_EPS = 1e-6
LANES = 128
HEAD_DIM = 128
HGRN_CHUNK = 64
HGRN_SUB = 16
N_GROUPS = 4
EXPERTS_PER_GROUP = 8
N_EXPERTS = N_GROUPS * EXPERTS_PER_GROUP
VMEM_LIMIT = 48 * 1024 * 1024
NEG_BIG = -1e30
LOG2E = 1.4426950408889634

_NT = (((1,), (1,)), ((), ()))
_TN = (((0,), (0,)), ((), ()))


def _params(sem):
    return pltpu.CompilerParams(dimension_semantics=sem, vmem_limit_bytes=VMEM_LIMIT)


def _sigmoid(x):
    return 1.0 / (1.0 + jnp.exp(-x))


def _split3(x):
    h = x.astype(BF16)
    r = x - h.astype(F32)
    m = r.astype(BF16)
    l = (r - m.astype(F32)).astype(BF16)
    return h, m, l


def _cat(parts, axis):
    return parts[0] if len(parts) == 1 else jnp.concatenate(parts, axis=axis)


def _inproj_kernel(x_ref, nw_ref, w_ref, o_ref):
    x = x_ref[...]
    ms = jnp.mean(x * x, axis=-1, keepdims=True)
    xn = ((x * lax.rsqrt(ms + NORM_EPS)) * nw_ref[...]).astype(BF16)
    o_ref[...] = jnp.dot(xn, w_ref[...], preferred_element_type=F32)


def _inproj(x, nw, w, tm, tn):
    n, d = x.shape
    c = w.shape[1]
    return pl.pallas_call(
        _inproj_kernel,
        grid=(c // tn, n // tm),
        in_specs=[pl.BlockSpec((tm, d), lambda j, i: (i, 0)),
                  pl.BlockSpec((1, d), lambda j, i: (0, 0)),
                  pl.BlockSpec((d, tn), lambda j, i: (0, j))],
        out_specs=pl.BlockSpec((tm, tn), lambda j, i: (i, j)),
        out_shape=jax.ShapeDtypeStruct((n, c), F32),
        compiler_params=_params(("arbitrary", "arbitrary")),
        name="inproj",
    )(x, nw, w)


def _hgrn_kernel(*refs, tb, chunk, sub, hp, has_state):
    if has_state:
        (hq_ref, hf_ref, hi_ref, hg_ref, lbl_ref, nw_ref, tri_ref, wsel_ref, s0_ref,
         o_ref, sout_ref, st_ref, q_scr, b_scr, c_scr) = refs
    else:
        (hq_ref, hf_ref, hi_ref, hg_ref, lbl_ref, nw_ref, tri_ref, wsel_ref,
         o_ref, sout_ref, st_ref, q_scr, b_scr, c_scr) = refs
    n_sub = chunk // sub
    ti = pl.program_id(2)
    hs = [slice(h * HEAD_DIM, (h + 1) * HEAD_DIM) for h in range(hp)]

    lg = lbl_ref[...]
    lg = jnp.exp(lg - jnp.max(lg, axis=0, keepdims=True))
    lb = lg[0:1, :] / jnp.sum(lg, axis=0, keepdims=True)

    @pl.when(ti == 0)
    def _():
        for h in range(hp):
            st_ref[h] = s0_ref[0, h].T if has_state else jnp.zeros((HEAD_DIM, HEAD_DIM), F32)

    row = lax.broadcasted_iota(jnp.int32, (chunk, chunk), 0)
    col = lax.broadcasted_iota(jnp.int32, (chunk, chunk), 1)
    blk_diff = row // sub - col // sub
    diag_mask = (blk_diff == 0) & (col <= row)
    diag_mask_all = _cat([diag_mask] * hp, 0)
    tri = tri_ref[...]
    wsel = wsel_ref[...]
    nw = nw_ref[...]

    def prepare(ci, slot):
        r0 = pl.multiple_of(ci * chunk, chunk)
        hq = hq_ref[pl.ds(r0, chunk), :]
        hf = hf_ref[pl.ds(r0, chunk), :]
        f = lb + (1.0 - lb) * _sigmoid(hf)
        k = 1.0 - f
        l1, l2, l3 = _split3(jnp.log(f) * LOG2E)
        b = (jnp.dot(tri, l1, preferred_element_type=F32) + jnp.dot(tri, l2, preferred_element_type=F32)
             + jnp.dot(tri, l3, preferred_element_type=F32))
        q_scr[slot] = hq * _sigmoid(hq)
        b_scr[slot] = b
        c_scr[slot] = b - jnp.log(k) * LOG2E

    def main(ci, slot):
        r0 = pl.multiple_of(ci * chunk, chunk)
        v = hi_ref[pl.ds(r0, chunk), :]
        hg = hg_ref[pl.ds(r0, chunk), :]
        q = q_scr[slot]
        b = b_scr[slot]
        c = c_scr[slot]
        b_rows = b_scr.at[slot]
        c_rows = c_scr.at[slot]
        vb = v.astype(BF16)
        b_last = b_rows[pl.ds(chunk - 1, 1), :]
        qd = (q * jnp.exp2(b)).astype(BF16)
        kd = jnp.exp2(b_last - c).astype(BF16)
        s_decay = jnp.exp2(b_last)

        slabs = []
        for r in range(sub):
            parts = []
            for i in range(n_sub):
                c_row = c_rows[pl.ds(i * sub + r, 1), :]
                sl = slice(i * sub, (i + 1) * sub)
                parts.append(q[sl] * jnp.exp2(jnp.minimum(b[sl] - c_row, 0.0)))
            slabs.append(_cat(parts, 0).astype(BF16))
        xcat = _cat([_cat([s[:, hs[h]] for s in slabs], 1) for h in range(hp)], 0)
        a_all = jnp.where(diag_mask_all, jnp.dot(xcat, wsel, preferred_element_type=F32), 0.0)

        if n_sub > 1:
            starts = [None] + [b_rows[pl.ds(i * sub - 1, 1), :] for i in range(1, n_sub)]
            ends = [b_rows[pl.ds(i * sub + sub - 1, 1), :] for i in range(n_sub)]
            qh, kh = [], []
            for i in range(n_sub):
                sl = slice(i * sub, (i + 1) * sub)
                qh.append(q[sl] if i == 0 else q[sl] * jnp.exp2(b[sl] - starts[i]))
                kh.append(jnp.exp2(ends[i] - c[sl]))
            khat = _cat(kh, 0).astype(BF16)
            qds = []
            for d in range(1, n_sub):
                parts = []
                for i in range(n_sub):
                    if i < d:
                        parts.append(jnp.zeros((sub, hp * HEAD_DIM), F32))
                    elif d == 1:
                        parts.append(qh[i])
                    else:
                        parts.append(qh[i] * jnp.exp2(starts[i] - ends[i - d]))
                qds.append(_cat(parts, 0).astype(BF16))
            qstack = _cat(qds, 0)

        outs = []
        for h in range(hp):
            st = st_ref[h]
            o = lax.dot_general(qd[:, hs[h]], st.astype(BF16), _NT, preferred_element_type=F32)
            st_ref[h] = st * s_decay[:, hs[h]] + lax.dot_general(vb[:, hs[h]], kd[:, hs[h]], _TN,
                                                                 preferred_element_type=F32)
            a = a_all[h * chunk:(h + 1) * chunk]
            if n_sub > 1:
                m = lax.dot_general(qstack[:, hs[h]], khat[:, hs[h]], _NT, preferred_element_type=F32)
                for d in range(1, n_sub):
                    a = a + jnp.where(blk_diff == d, m[(d - 1) * chunk:d * chunk], 0.0)
            o = o + jnp.dot(a.astype(BF16), vb[:, hs[h]], preferred_element_type=F32)
            ms = jnp.mean(o * o, axis=-1, keepdims=True)
            outs.append((o * lax.rsqrt(ms + NORM_EPS)) * nw[:, hs[h]])
        g = hg * _sigmoid(hg)
        o_ref[pl.ds(r0, chunk), :] = (_cat(outs, 1) * g).astype(o_ref.dtype)

    n_chunks = tb // chunk
    prepare(0, 0)
    if n_chunks == 1:
        main(0, 0)
    else:
        def pair(pp, carry):
            ci = 2 * pp
            prepare(ci + 1, 1)
            main(ci, 0)
            prepare(jnp.minimum(ci + 2, n_chunks - 1), 0)
            main(ci + 1, 1)
            return carry

        lax.fori_loop(0, n_chunks // 2, pair, 0)

    @pl.when(ti == pl.num_programs(2) - 1)
    def _():
        for h in range(hp):
            sout_ref[0, h] = st_ref[h].T


def _hgrn_consts(chunk, sub):
    t = np.arange(chunk)
    tri = (t[None, :] <= t[:, None]).astype(np.float32)
    rows = np.arange(sub * HEAD_DIM) // HEAD_DIM
    wsel = (rows[:, None] == (t[None, :] % sub)).astype(np.float32)
    return jnp.asarray(tri, BF16), jnp.asarray(wsel, BF16)


def _hgrn(proj, lb_logits, norm_w, state, batch, seq, heads, col0, hp, tb):
    chunk = min(HGRN_CHUNK, seq)
    sub = min(HGRN_SUB, chunk)
    tri, wsel = _hgrn_consts(chunk, sub)
    has_state = state is not None
    nt = seq // tb
    w = hp * HEAD_DIM

    def col_spec(g):
        return pl.BlockSpec((tb, w), lambda b, h, t: (b * nt + t, (col0 + g * heads) // hp + h))

    const = lambda b, h, t: (0, 0)
    in_specs = [col_spec(0), col_spec(1), col_spec(2), col_spec(3),
                pl.BlockSpec((lb_logits.shape[0], w), lambda b, h, t: (0, h)),
                pl.BlockSpec((1, w), const),
                pl.BlockSpec(tri.shape, const),
                pl.BlockSpec(wsel.shape, const)]
    args = [proj, proj, proj, proj, lb_logits, jnp.tile(norm_w, (1, hp)), tri, wsel]
    if has_state:
        in_specs.append(pl.BlockSpec((1, hp, HEAD_DIM, HEAD_DIM), lambda b, h, t: (b, h, 0, 0)))
        args.append(state)
    return pl.pallas_call(
        functools.partial(_hgrn_kernel, tb=tb, chunk=chunk, sub=sub, hp=hp, has_state=has_state),
        grid=(batch, heads // hp, nt),
        in_specs=in_specs,
        out_specs=[pl.BlockSpec((tb, w), lambda b, h, t: (b * nt + t, h)),
                   pl.BlockSpec((1, hp, HEAD_DIM, HEAD_DIM), lambda b, h, t: (b, h, 0, 0))],
        out_shape=[jax.ShapeDtypeStruct((batch * seq, heads * HEAD_DIM), BF16),
                   jax.ShapeDtypeStruct((batch, heads, HEAD_DIM, HEAD_DIM), F32)],
        scratch_shapes=[pltpu.VMEM((hp, HEAD_DIM, HEAD_DIM), F32),
                        pltpu.VMEM((2, chunk, w), F32),
                        pltpu.VMEM((2, chunk, w), F32),
                        pltpu.VMEM((2, chunk, w), F32)],
        compiler_params=_params(("arbitrary", "arbitrary", "arbitrary")),
        name="hgrn_state" if has_state else "hgrn",
    )(*args)


def _sb_blocks(qs, ks, vs, u, c, mask):
    tq = qs[0].shape[0]
    z = _cat([lax.dot_general(q, k, _NT, preferred_element_type=F32) for q, k in zip(qs, ks)], 0)
    nz = -z
    sp = jnp.maximum(nz, 0.0) + jnp.log2(1.0 + jnp.exp2(jnp.minimum(z, nz)))
    l = nz - sp
    if mask is not None:
        l = jnp.where(mask, l, 0.0)
    s = jnp.dot(l.astype(BF16), u, preferred_element_type=F32)
    p = jnp.exp2(s + c - sp)
    if mask is not None:
        p = jnp.where(mask, p, 0.0)
    p = p.astype(BF16)
    pv = _cat([jnp.dot(p[h * tq:(h + 1) * tq], vs[h], preferred_element_type=F32) for h in range(len(qs))], 0)
    return pv, c + (s[:, 0:1] + l[:, 0:1])


def _sb_suffix(tk):
    s = np.arange(tk)
    u = (s[:, None] > s[None, :]).astype(np.float32)
    return jnp.asarray(u, BF16)


def _sb_prompt_kernel(q_ref, k_ref, v_ref, u_ref, o_ref, ko_ref, vo_ref, acc_ref, c_ref, kb_ref, vb_ref,
                      lc_ref, sp_ref, l0_ref, *, tq, hp, scale):
    i = pl.program_id(2)
    hs = [slice(h * HEAD_DIM, (h + 1) * HEAD_DIM) for h in range(hp)]

    for h in range(hp):
        ko_ref[0, h] = k_ref[:, hs[h]]
        vo_ref[0, h] = v_ref[:, hs[h]]
    own = pl.ds(pl.multiple_of(i * tq, tq), tq)
    kb_ref[own, :] = k_ref[...].astype(BF16)
    vb_ref[own, :] = v_ref[...].astype(BF16)

    q = (q_ref[...] * scale).astype(BF16)
    usuf = u_ref[...]

    def scores(j, slot, diagonal):
        kb = kb_ref[pl.ds(pl.multiple_of(j * tq, tq), tq), :]
        z = _cat([lax.dot_general(q[:, hs[h]], kb[:, hs[h]], _NT, preferred_element_type=F32) for h in range(hp)], 0)
        nz = -z
        sp = jnp.maximum(nz, 0.0) + jnp.log2(1.0 + jnp.exp2(jnp.minimum(z, nz)))
        l = nz - sp
        if diagonal:
            row = lax.broadcasted_iota(jnp.int32, (tq, tq), 0)
            col = lax.broadcasted_iota(jnp.int32, (tq, tq), 1)
            mask = _cat([col < row] * hp, 0)
            l = jnp.where(mask, l, 0.0)
            sp = jnp.where(mask, sp, -NEG_BIG)
        lc_ref[slot] = l.astype(BF16)
        sp_ref[slot] = sp
        l0_ref[slot] = l[:, 0:1]

    def values(j, slot):
        vb = vb_ref[pl.ds(pl.multiple_of(j * tq, tq), tq), :]
        s = jnp.dot(lc_ref[slot], usuf, preferred_element_type=F32)
        c = c_ref[...]
        p = jnp.exp2(s + c - sp_ref[slot]).astype(BF16)
        acc_ref[...] += _cat([jnp.dot(p[h * tq:(h + 1) * tq], vb[:, hs[h]], preferred_element_type=F32)
                              for h in range(hp)], 0)
        c_ref[...] = c + (s[:, 0:1] + l0_ref[slot])

    acc_ref[...] = jnp.zeros_like(acc_ref)
    c_ref[...] = jnp.zeros_like(c_ref)
    scores(i, 0, True)

    def pair(pp, carry):
        j = i - 1 - 2 * pp
        scores(j, 1, False)
        values(j + 1, 0)
        scores(j - 1, 0, False)
        values(j, 1)
        return carry

    lax.fori_loop(0, i // 2, pair, 0)

    @pl.when(i % 2 == 1)
    def _():
        scores(0, 1, False)
        values(1, 0)
        values(0, 1)

    @pl.when(i % 2 == 0)
    def _():
        values(0, 0)
    o_ref[...] = _cat([acc_ref[pl.ds(h * tq, tq), :] for h in range(hp)], 1).astype(o_ref.dtype)


def _sb_prompt(proj, batch, seq, heads, col0, tq, hp):
    nq = seq // tq
    w = hp * HEAD_DIM
    usuf = _sb_suffix(tq)
    kv_shape = jax.ShapeDtypeStruct((batch, heads, seq, HEAD_DIM), F32)
    return pl.pallas_call(
        functools.partial(_sb_prompt_kernel, tq=tq, hp=hp, scale=HEAD_DIM ** -0.5 * LOG2E),
        grid=(batch, heads // hp, nq),
        in_specs=[pl.BlockSpec((tq, w), lambda b, h, i: (b * nq + i, col0 // hp + h)),
                  pl.BlockSpec((tq, w), lambda b, h, i: (b * nq + i, (col0 + heads) // hp + h)),
                  pl.BlockSpec((tq, w), lambda b, h, i: (b * nq + i, (col0 + 2 * heads) // hp + h)),
                  pl.BlockSpec(usuf.shape, lambda b, h, i: (0, 0))],
        out_specs=[pl.BlockSpec((tq, w), lambda b, h, i: (b * nq + i, h)),
                   pl.BlockSpec((1, hp, tq, HEAD_DIM), lambda b, h, i: (b, h, i, 0)),
                   pl.BlockSpec((1, hp, tq, HEAD_DIM), lambda b, h, i: (b, h, i, 0))],
        out_shape=[jax.ShapeDtypeStruct((batch * seq, heads * HEAD_DIM), BF16), kv_shape, kv_shape],
        scratch_shapes=[pltpu.VMEM((hp * tq, HEAD_DIM), F32), pltpu.VMEM((hp * tq, 1), F32),
                        pltpu.VMEM((seq, w), BF16), pltpu.VMEM((seq, w), BF16),
                        pltpu.VMEM((2, hp * tq, tq), BF16), pltpu.VMEM((2, hp * tq, tq), F32),
                        pltpu.VMEM((2, hp * tq, 1), F32)],
        compiler_params=_params(("arbitrary", "arbitrary", "arbitrary")),
        name="sb_prompt",
    )(proj, proj, proj, usuf)


def _sb_decode_kernel(q_ref, k_ref, v_ref, pk_ref, pv_ref, un_ref, up_ref, o_ref, acc_ref, c_ref,
                      *, seq, heads, tk, scale):
    j = pl.program_id(1)
    hs = [slice(h * HEAD_DIM, (h + 1) * HEAD_DIM) for h in range(heads)]
    q = (q_ref[...] * scale).astype(BF16)
    qs = [q[:, hs[h]] for h in range(heads)]

    @pl.when(j == 0)
    def _():
        pad = jnp.zeros((LANES - seq, heads * HEAD_DIM), F32)
        kn = jnp.concatenate([k_ref[...], pad], axis=0).astype(BF16)
        vn = jnp.concatenate([v_ref[...], pad], axis=0).astype(BF16)
        row = lax.broadcasted_iota(jnp.int32, (seq, LANES), 0)
        col = lax.broadcasted_iota(jnp.int32, (seq, LANES), 1)
        mask = _cat([col < row] * heads, 0)
        pv, c = _sb_blocks(qs, [kn[:, hs[h]] for h in range(heads)], [vn[:, hs[h]] for h in range(heads)],
                           un_ref[...], jnp.zeros((heads * seq, 1), F32), mask)
        acc_ref[...] = pv
        c_ref[...] = c

    usuf = up_ref[...]
    for sblk in reversed(range(pk_ref.shape[2] // tk)):
        ks = [pk_ref[0, h, pl.ds(sblk * tk, tk), :].astype(BF16) for h in range(heads)]
        vs = [pv_ref[0, h, pl.ds(sblk * tk, tk), :].astype(BF16) for h in range(heads)]
        pv, c = _sb_blocks(qs, ks, vs, usuf, c_ref[...], None)
        acc_ref[...] += pv
        c_ref[...] = c

    @pl.when(j == pl.num_programs(1) - 1)
    def _():
        o_ref[...] = _cat([acc_ref[pl.ds(h * seq, seq), :] for h in range(heads)], 1).astype(o_ref.dtype)


def _sb_decode(proj, past_k, past_v, batch, seq, heads, col0, tk, tkb):
    past = past_k.shape[2]
    nblk = past // tkb
    w = heads * HEAD_DIM
    un, up = _sb_suffix(LANES), _sb_suffix(tk)
    kv_spec = pl.BlockSpec((1, heads, tkb, HEAD_DIM), lambda b, j: (b, 0, nblk - 1 - j, 0))
    return pl.pallas_call(
        functools.partial(_sb_decode_kernel, seq=seq, heads=heads, tk=tk, scale=HEAD_DIM ** -0.5 * LOG2E),
        grid=(batch, nblk),
        in_specs=[pl.BlockSpec((seq, w), lambda b, j: (b, col0 // heads)),
                  pl.BlockSpec((seq, w), lambda b, j: (b, col0 // heads + 1)),
                  pl.BlockSpec((seq, w), lambda b, j: (b, col0 // heads + 2)),
                  kv_spec, kv_spec,
                  pl.BlockSpec(un.shape, lambda b, j: (0, 0)),
                  pl.BlockSpec(up.shape, lambda b, j: (0, 0))],
        out_specs=pl.BlockSpec((seq, w), lambda b, j: (b, 0)),
        out_shape=jax.ShapeDtypeStruct((batch * seq, w), BF16),
        scratch_shapes=[pltpu.VMEM((heads * seq, HEAD_DIM), F32), pltpu.VMEM((heads * seq, 1), F32)],
        compiler_params=_params(("arbitrary", "arbitrary")),
        name="sb_decode",
    )(proj, proj, proj, past_k, past_v, un, up)


def _route(logits):
    lane = lax.broadcasted_iota(jnp.int32, logits.shape, 1).astype(F32)
    first = lambda hit: jnp.min(jnp.where(hit, lane, float(LANES)), axis=-1, keepdims=True)
    is_g = lane < N_GROUPS
    gl = jnp.where(is_g, logits, NEG_BIG)
    gmax = jnp.max(gl, axis=-1, keepdims=True)
    gidx = first(gl == gmax)
    gsum = jnp.sum(jnp.where(is_g, jnp.exp(gl - gmax), 0.0), axis=-1, keepdims=True)
    gw = 1.0 / gsum
    lo = N_GROUPS + EXPERTS_PER_GROUP * gidx
    in_g = (lane >= lo) & (lane < lo + EXPERTS_PER_GROUP)
    el = jnp.where(in_g, logits, NEG_BIG)
    m1 = jnp.max(el, axis=-1, keepdims=True)
    i1 = first(in_g & (el == m1))
    rest = in_g & (lane != i1)
    el2 = jnp.where(rest, logits, NEG_BIG)
    m2 = jnp.max(el2, axis=-1, keepdims=True)
    i2 = first(rest & (el2 == m2))
    t = jnp.exp(m2 - m1)
    w0 = gw / (1.0 + t)
    w1 = gw * t / (1.0 + t)
    e0 = i1 - N_GROUPS
    e1 = i2 - N_GROUPS
    return jnp.where(lane == 0, e0, jnp.where(lane == 1, e1, jnp.where(lane == 2, w0, jnp.where(lane == 3, w1, 0.0))))


def _outproj_kernel(x_ref, a_ref, b_ref, wa_ref, wb_ref, nw_ref, rw_ref, rb_ref, *rest):
    x2_ref, xn_ref, r_ref = rest[-3:]
    wh = rw_ref[0]
    half = min(x_ref.shape[0], 256)
    for h0 in range(0, x_ref.shape[0], half):
        rows = pl.ds(h0, half)
        acc = jnp.dot(a_ref[rows, :], wa_ref[...], preferred_element_type=F32)
        acc = acc + jnp.dot(b_ref[rows, :], wb_ref[...], preferred_element_type=F32)
        x2 = x_ref[rows, :] + acc
        x2_ref[rows, :] = x2
        ms = jnp.mean(x2 * x2, axis=-1, keepdims=True)
        xn = (x2 * lax.rsqrt(ms + NORM_EPS)) * nw_ref[...]
        xn_ref[rows, :] = xn
        xh = xn.astype(BF16)
        xl = (xn - xh.astype(F32)).astype(BF16)
        logits = (jnp.dot(xh, wh, preferred_element_type=F32) + jnp.dot(xh, rw_ref[1], preferred_element_type=F32)
                  + jnp.dot(xl, wh, preferred_element_type=F32)) + rb_ref[...]
        r_ref[rows, :] = _route(logits)


def _outproj(x, oa, ob, wa, wb, nw, rw, rb, tm, n_total, row0, prev):
    n, d = x.shape
    blk0 = row0 // tm
    row = lambda i: (i, 0)
    orow = lambda i: (blk0 + i, 0)
    const = lambda i: (0, 0)
    in_specs = [pl.BlockSpec((tm, d), row),
                pl.BlockSpec((tm, oa.shape[1]), row),
                pl.BlockSpec((tm, ob.shape[1]), row),
                pl.BlockSpec(wa.shape, const),
                pl.BlockSpec(wb.shape, const),
                pl.BlockSpec((1, d), const),
                pl.BlockSpec(rw.shape, lambda i: (0, 0, 0)),
                pl.BlockSpec((1, LANES), const)]
    args = [x, oa, ob, wa, wb, nw, rw, rb]
    aliases = {}
    if prev is not None:
        in_specs += [pl.BlockSpec(memory_space=pl.ANY)] * 3
        aliases = {len(args) + t: t for t in range(3)}
        args += list(prev)
    return pl.pallas_call(
        _outproj_kernel,
        grid=(n // tm,),
        in_specs=in_specs,
        out_specs=[pl.BlockSpec((tm, d), orow), pl.BlockSpec((tm, d), orow), pl.BlockSpec((tm, LANES), orow)],
        out_shape=[jax.ShapeDtypeStruct((n_total, d), F32), jax.ShapeDtypeStruct((n_total, d), F32),
                   jax.ShapeDtypeStruct((n_total, LANES), F32)],
        input_output_aliases=aliases,
        compiler_params=_params(("arbitrary",)),
        name="outproj",
    )(*args)


def _row_out(x_ref, r, dst_hbm, row, sem):
    return pltpu.make_async_copy(x_ref.at[pl.ds(r, 1), :], dst_hbm.at[pl.ds(row, 1), :], sem)


def _zero_fill_copies(z_ref, xs_hbm, start, length, sem):
    sub = 8
    head = jnp.minimum((-start) & (sub - 1), length)
    body0 = start + head
    body = length - head
    tail0 = body0 + (body & -sub)
    row = lambda off: pltpu.make_async_copy(z_ref.at[pl.ds(0, 1), :], xs_hbm.at[pl.ds(off, 1), :], sem)
    pairs = [(k < head, row(start + k)) for k in range(sub - 1)]
    bit = z_ref.shape[0]
    while bit >= sub:
        off = pl.multiple_of(body0 + (body & (-2 * bit)), sub)
        pairs.append(((body & bit) != 0,
                      pltpu.make_async_copy(z_ref.at[pl.ds(0, bit), :], xs_hbm.at[pl.ds(off, bit), :], sem)))
        bit //= 2
    pairs += [(k < (body & (sub - 1)), row(tail0 + k)) for k in range(sub - 1)]
    return pairs


def _scatter_kernel(ps_ref, pl_ref, pos_ref, x_ref, xs_hbm, z_ref, sem, zsem, *, tm):
    @pl.when(pl.program_id(0) == 0)
    def _():
        z_ref[...] = jnp.zeros_like(z_ref)

        def fill(e, carry):
            for pred, cp in _zero_fill_copies(z_ref, xs_hbm, ps_ref[e], pl_ref[e], zsem):
                pl.when(pred)(cp.start)
            return carry
        lax.fori_loop(0, N_EXPERTS, fill, 0)

        def settle(e, carry):
            for pred, cp in _zero_fill_copies(z_ref, xs_hbm, ps_ref[e], pl_ref[e], zsem):
                pl.when(pred)(cp.wait)
            return carry
        lax.fori_loop(0, N_EXPERTS, settle, 0)

    def issue(r, carry):
        _row_out(x_ref, r, xs_hbm, pos_ref[0, 0, 2 * r], sem).start()
        _row_out(x_ref, r, xs_hbm, pos_ref[0, 0, 2 * r + 1], sem).start()
        return carry
    lax.fori_loop(0, tm, issue, 0, unroll=8)

    def drain(r, carry):
        _row_out(x_ref, r, xs_hbm, 0, sem).wait()
        _row_out(x_ref, r, xs_hbm, 0, sem).wait()
        return carry
    lax.fori_loop(0, tm, drain, 0, unroll=8)


def _scatter(pad_start, pad_len, pos, x, tm, rows, tm_rows):
    n, d = x.shape
    return pl.pallas_call(
        functools.partial(_scatter_kernel, tm=tm),
        grid_spec=pltpu.PrefetchScalarGridSpec(
            num_scalar_prefetch=2,
            grid=(n // tm,),
            in_specs=[pl.BlockSpec((1, 1, 2 * tm), lambda m, *_: (m, 0, 0), memory_space=pltpu.SMEM),
                      pl.BlockSpec((tm, d), lambda m, *_: (m, 0))],
            out_specs=pl.BlockSpec(memory_space=pl.ANY),
            scratch_shapes=[pltpu.VMEM((tm_rows // 2, d), x.dtype), pltpu.SemaphoreType.DMA(()),
                            pltpu.SemaphoreType.DMA(())]),
        out_shape=jax.ShapeDtypeStruct((rows, d), x.dtype),
        compiler_params=_params(("arbitrary",)),
        name="scatter",
    )(pad_start, pad_len, pos, x)


def _weight_copies(e, slot, w_hbm, w_stage, sem):
    return [pltpu.make_async_copy(w_hbm[t].at[e], w_stage[t].at[slot], sem.at[slot, t]) for t in range(3)]


def _experts_kernel(te_ref, nu_ref, nx_ref, ig_ref, x_ref, wg_hbm, wu_hbm, wd_hbm, o_ref,
                    sg, su, sd, g_scr, u_scr, d_scr, sem):
    m = pl.program_id(0)
    w_hbm, w_stage = (wg_hbm, wu_hbm, wd_hbm), (sg, su, sd)

    @pl.when(m < nu_ref[0])
    def _():
        e = te_ref[m]
        new_expert = (m == 0) | (e != te_ref[jnp.maximum(m - 1, 0)])
        slot = ig_ref[m] % 2

        @pl.when(m == 0)
        def _():
            for cp in _weight_copies(e, slot, w_hbm, w_stage, sem):
                cp.start()

        @pl.when(new_expert)
        def _():
            for cp in _weight_copies(e, slot, w_hbm, w_stage, sem):
                cp.wait()

            @pl.when(nx_ref[m] != e)
            def _():
                for cp in _weight_copies(nx_ref[m], 1 - slot, w_hbm, w_stage, sem):
                    cp.start()

            g_scr[...] = sg[slot].astype(BF16)
            u_scr[...] = su[slot].astype(BF16)
            d_scr[...] = sd[slot].astype(BF16)

        x = x_ref[...].astype(BF16)
        g = jnp.dot(x, g_scr[...], preferred_element_type=F32)
        u = jnp.dot(x, u_scr[...], preferred_element_type=F32)
        hid = ((g * _sigmoid(g)) * u).astype(BF16)
        o_ref[...] = jnp.dot(hid, d_scr[...], preferred_element_type=F32)


def _experts(tile_expert, n_used, tile_next, tile_group, xs, wg, wu, wd, tm):
    rows, d = xs.shape
    ff = wg.shape[2]
    tile = lambda m, te, nu, *_: (jnp.minimum(m, jnp.maximum(nu[0] - 1, 0)), 0)
    hbm = pl.BlockSpec(memory_space=pl.ANY)
    return pl.pallas_call(
        _experts_kernel,
        grid_spec=pltpu.PrefetchScalarGridSpec(
            num_scalar_prefetch=4,
            grid=(rows // tm,),
            in_specs=[pl.BlockSpec((tm, d), tile), hbm, hbm, hbm],
            out_specs=pl.BlockSpec((tm, d), tile),
            scratch_shapes=[pltpu.VMEM((2, d, ff), F32), pltpu.VMEM((2, d, ff), F32), pltpu.VMEM((2, ff, d), F32),
                            pltpu.VMEM((d, ff), BF16), pltpu.VMEM((d, ff), BF16), pltpu.VMEM((ff, d), BF16),
                            pltpu.SemaphoreType.DMA((2, 3))]),
        out_shape=jax.ShapeDtypeStruct((rows, d), F32),
        compiler_params=_params(("arbitrary",)),
        name="experts",
    )(tile_expert, n_used, tile_next, tile_group, xs, wg, wu, wd)


def _row_in(src_hbm, row, dst, r, sem):
    return pltpu.make_async_copy(src_hbm.at[pl.ds(row, 1), :], dst.at[pl.ds(r, 1), :], sem)


def _combine_kernel(pos_ref, nxt_ref, x_ref, r_ref, fw_ref, ys_hbm, o_ref, buf, sem, *, tm, ntiles):
    m = pl.program_id(0)
    slot = m % 2

    def issue(tbl, s):
        def go(r, carry):
            _row_in(ys_hbm, tbl[0, 0, 2 * r], buf.at[s, 0], r, sem.at[s]).start()
            _row_in(ys_hbm, tbl[0, 0, 2 * r + 1], buf.at[s, 1], r, sem.at[s]).start()
            return carry
        lax.fori_loop(0, tm, go, 0, unroll=8)

    @pl.when(m == 0)
    def _():
        issue(pos_ref, 0)

    @pl.when(m + 1 < ntiles)
    def _():
        issue(nxt_ref, 1 - slot)

    def drain(r, carry):
        _row_in(ys_hbm, 0, buf.at[slot, 0], r, sem.at[slot]).wait()
        _row_in(ys_hbm, 0, buf.at[slot, 1], r, sem.at[slot]).wait()
        return carry
    lax.fori_loop(0, tm, drain, 0, unroll=8)
    rt = r_ref[...]
    x = x_ref[...] + (buf[slot, 0] * rt[:, 2:3] + buf[slot, 1] * rt[:, 3:4])
    ms = jnp.mean(x * x, axis=-1, keepdims=True)
    o_ref[...] = (x * lax.rsqrt(ms + NORM_EPS)) * fw_ref[...]


def _combine(pos, x2, r, fw, ys, tm, row0, n):
    d = x2.shape[1]
    blk0 = row0 // tm
    ntiles = n // tm
    smem = functools.partial(pl.BlockSpec, memory_space=pltpu.SMEM)
    return pl.pallas_call(
        functools.partial(_combine_kernel, tm=tm, ntiles=ntiles),
        grid=(ntiles,),
        in_specs=[smem((1, 1, 2 * tm), lambda m: (blk0 + m, 0, 0)),
                  smem((1, 1, 2 * tm), lambda m: (blk0 + jnp.minimum(m + 1, ntiles - 1), 0, 0)),
                  pl.BlockSpec((tm, d), lambda m: (blk0 + m, 0)),
                  pl.BlockSpec((tm, LANES), lambda m: (blk0 + m, 0)),
                  pl.BlockSpec((1, d), lambda m: (0, 0)),
                  pl.BlockSpec(memory_space=pl.ANY)],
        out_specs=pl.BlockSpec((tm, d), lambda m: (m, 0)),
        out_shape=jax.ShapeDtypeStruct((n, d), F32),
        scratch_shapes=[pltpu.VMEM((2, 2, tm, d), F32), pltpu.SemaphoreType.DMA((2,))],
        compiler_params=_params(("arbitrary",)),
        name="combine",
    )(pos, pos, x2, r, fw, ys)


def _dispatch_tables(r, tm):
    n = r.shape[0]
    tiles = (2 * n) // tm + N_EXPERTS
    ids = jnp.arange(N_EXPERTS, dtype=jnp.int32)
    e = r[:, 0:2].astype(jnp.int32).reshape(-1)
    onehot = (e[:, None] == ids[None, :]).astype(jnp.int32)
    incl = jnp.cumsum(onehot, axis=0)
    counts = incl[-1]
    etiles = (counts + tm - 1) // tm
    tile_end = jnp.cumsum(etiles)
    tile_start = tile_end - etiles
    n_used = tile_end[-1]
    pos = jnp.sum(onehot * (incl - 1 + (tile_start * tm)[None, :]), axis=1)
    m = jnp.minimum(jnp.arange(tiles, dtype=jnp.int32), n_used - 1)
    te = jnp.sum((tile_end[None, :] <= m[:, None]).astype(jnp.int32), axis=1)
    sel = (te[:, None] == ids[None, :]).astype(jnp.int32)
    pick = lambda tbl: jnp.sum(sel * tbl[None, :], axis=1)
    used = counts > 0
    later = used[None, :] & (ids[None, :] > ids[:, None])
    next_used = jnp.min(jnp.where(later, ids[None, :], N_EXPERTS), axis=1)
    next_used = jnp.where(next_used == N_EXPERTS, ids, next_used)
    group = jnp.cumsum(used.astype(jnp.int32)) - 1
    i32 = lambda a: a.astype(jnp.int32)
    return (i32(pos), i32(te), i32(n_used).reshape(1), i32(pick(next_used)), i32(pick(group)),
            i32(tile_start * tm + counts), i32(etiles * tm - counts))


def kernel(x_prompt, x_sample, cache_sb_k, cache_sb_v, state_hgrn, w_in, hg_lb_logits, hg_norm_w, w_out, norm1_w,
           norm2_w, router_group_w, router_group_b, router_expert_w, router_expert_b, expert_w_gate, expert_w_up,
           expert_w_down, final_norm_w):
    bp, tp, d = x_prompt.shape
    bs, ts, _ = x_sample.shape
    np_, ns = bp * tp, bs * ts
    n = np_ + ns
    heads = w_in.shape[2] // (7 * HEAD_DIM)
    hw = heads * HEAD_DIM
    tm_tok = 256
    tm_moe = 256

    w_in_b = w_in[0].astype(BF16)
    w_out_b = w_out[0].astype(BF16)
    n1 = norm1_w[0].reshape(1, d)
    n2 = norm2_w[0].reshape(1, d)
    hnw = hg_norm_w[0].reshape(1, HEAD_DIM)
    fw = final_norm_w.reshape(1, d)

    xp = x_prompt.reshape(np_, d)
    xs = x_sample.reshape(ns, d)
    proj_p = _inproj(xp, n1, w_in_b, 256, w_in_b.shape[1] // 2)
    proj_s = _inproj(xs, n1, w_in_b, ns, w_in_b.shape[1] // 2)

    ohg_p, st_p = _hgrn(proj_p, hg_lb_logits, hnw, None, bp, tp, heads, 0, 4, 512)
    ohg_s, st_s = _hgrn(proj_s, hg_lb_logits, hnw, state_hgrn[0], bs, ts, heads, 0, heads, ts)
    osb_p, k_p, v_p = _sb_prompt(proj_p, bp, tp, heads, 4 * heads, 256, 4)
    osb_s = _sb_decode(proj_s, cache_sb_k[0], cache_sb_v[0], bs, ts, heads, 4 * heads, 512, 1024)

    rw = jnp.concatenate([router_group_w[0], router_expert_w[0]], axis=1)
    rw = jnp.pad(rw, ((0, 0), (0, LANES - rw.shape[1])))
    rwh = rw.astype(BF16)
    rw2 = jnp.stack([rwh, (rw - rwh.astype(F32)).astype(BF16)])
    rb = jnp.concatenate([router_group_b[0], router_expert_b[0]])
    rb = jnp.pad(rb, (0, LANES - rb.shape[0])).reshape(1, LANES)

    wa, wb = w_out_b[:hw], w_out_b[hw:]
    outs = _outproj(xp, ohg_p, osb_p, wa, wb, n2, rw2, rb, 2 * tm_tok, n, 0, None)
    x2, xn2, r = _outproj(xs, ohg_s, osb_s, wa, wb, n2, rw2, rb, tm_tok, n, np_, outs)

    pos, tile_expert, n_used, tile_next, tile_group, pad_start, pad_len = _dispatch_tables(r, tm_moe)
    pos = pos.reshape(n // tm_tok, 1, 2 * tm_tok)
    xg = _scatter(pad_start, pad_len, pos, xn2, tm_tok, tile_expert.shape[0] * tm_moe, tm_moe)
    ys = _experts(tile_expert, n_used, tile_next, tile_group, xg,
                  expert_w_gate[0], expert_w_up[0], expert_w_down[0], tm_moe)
    y_p = _combine(pos, x2, r, fw, ys, tm_tok, 0, np_)
    y_s = _combine(pos, x2, r, fw, ys, tm_tok, np_, ns)

    heads_s = lambda a: a.reshape(bs, ts, heads, HEAD_DIM).transpose(0, 2, 1, 3)[None]
    k_s = heads_s(proj_s[:, 5 * hw:6 * hw])
    v_s = heads_s(proj_s[:, 6 * hw:7 * hw])
    return (y_p.reshape(bp, tp, d), y_s.reshape(bs, ts, d), k_p[None], v_p[None], st_p[None],
            k_s, v_s, st_s[None])
```

```python
import functools

import jax
import jax.numpy as jnp
import numpy as np
from jax import lax
from jax.experimental import pallas as pl
from jax.experimental.pallas import tpu as pltpu

F32 = jnp.float32
BF16 = jnp.bfloat16
NORM_EPS = 1e-6
LANES = 128
HEAD_DIM = 128
HGRN_CHUNK = 64
HGRN_SUB = 16
N_GROUPS = 4
EXPERTS_PER_GROUP = 8
N_EXPERTS = N_GROUPS * EXPERTS_PER_GROUP
VMEM_LIMIT = 48 * 1024 * 1024
NEG_BIG = -1e30
LOG2E = 1.4426950408889634

_NT = (((1,), (1,)), ((), ()))
_TN = (((0,), (0,)), ((), ()))


def _params(sem):
    return pltpu.CompilerParams(dimension_semantics=sem, vmem_limit_bytes=VMEM_LIMIT)


def _sigmoid(x):
    return 1.0 / (1.0 + jnp.exp(-x))


def _split3(x):
    h = x.astype(BF16)
    r = x - h.astype(F32)
    m = r.astype(BF16)
    l = (r - m.astype(F32)).astype(BF16)
    return h, m, l


def _cat(parts, axis):
    return parts[0] if len(parts) == 1 else jnp.concatenate(parts, axis=axis)


def _inproj_kernel(x_ref, nw_ref, w_ref, o_ref):
    x = x_ref[...]
    ms = jnp.mean(x * x, axis=-1, keepdims=True)
    xn = ((x * lax.rsqrt(ms + NORM_EPS)) * nw_ref[...]).astype(BF16)
    o_ref[...] = jnp.dot(xn, w_ref[...], preferred_element_type=F32)


def _inproj(x, nw, w, tm, tn):
    n, d = x.shape
    c = w.shape[1]
    return pl.pallas_call(
        _inproj_kernel,
        grid=(c // tn, n // tm),
        in_specs=[pl.BlockSpec((tm, d), lambda j, i: (i, 0)),
                  pl.BlockSpec((1, d), lambda j, i: (0, 0)),
                  pl.BlockSpec((d, tn), lambda j, i: (0, j))],
        out_specs=pl.BlockSpec((tm, tn), lambda j, i: (i, j)),
        out_shape=jax.ShapeDtypeStruct((n, c), F32),
        compiler_params=_params(("arbitrary", "arbitrary")),
        name="inproj",
    )(x, nw, w)


def _hgrn_kernel(*refs, tb, chunk, sub, hp, has_state):
    if has_state:
        (hq_ref, hf_ref, hi_ref, hg_ref, lbl_ref, nw_ref, tri_ref, wsel_ref, s0_ref,
         o_ref, sout_ref, st_ref, q_scr, b_scr, c_scr) = refs
    else:
        (hq_ref, hf_ref, hi_ref, hg_ref, lbl_ref, nw_ref, tri_ref, wsel_ref,
         o_ref, sout_ref, st_ref, q_scr, b_scr, c_scr) = refs
    n_sub = chunk // sub
    ti = pl.program_id(2)
    hs = [slice(h * HEAD_DIM, (h + 1) * HEAD_DIM) for h in range(hp)]

    lg = lbl_ref[...]
    lg = jnp.exp(lg - jnp.max(lg, axis=0, keepdims=True))
    lb = lg[0:1, :] / jnp.sum(lg, axis=0, keepdims=True)

    @pl.when(ti == 0)
    def _():
        for h in range(hp):
            st_ref[h] = s0_ref[0, h].T if has_state else jnp.zeros((HEAD_DIM, HEAD_DIM), F32)

    row = lax.broadcasted_iota(jnp.int32, (chunk, chunk), 0)
    col = lax.broadcasted_iota(jnp.int32, (chunk, chunk), 1)
    blk_diff = row // sub - col // sub
    diag_mask = (blk_diff == 0) & (col <= row)
    diag_mask_all = _cat([diag_mask] * hp, 0)
    tri = tri_ref[...]
    wsel = wsel_ref[...]
    nw = nw_ref[...]

    def prepare(ci, slot):
        r0 = pl.multiple_of(ci * chunk, chunk)
        hq = hq_ref[pl.ds(r0, chunk), :]
        hf = hf_ref[pl.ds(r0, chunk), :]
        f = lb + (1.0 - lb) * _sigmoid(hf)
        k = 1.0 - f
        l1, l2, l3 = _split3(jnp.log(f) * LOG2E)
        b = (jnp.dot(tri, l1, preferred_element_type=F32) + jnp.dot(tri, l2, preferred_element_type=F32)
             + jnp.dot(tri, l3, preferred_element_type=F32))
        q_scr[slot] = hq * _sigmoid(hq)
        b_scr[slot] = b
        c_scr[slot] = b - jnp.log(k) * LOG2E

    def main(ci, slot):
        r0 = pl.multiple_of(ci * chunk, chunk)
        v = hi_ref[pl.ds(r0, chunk), :]
        hg = hg_ref[pl.ds(r0, chunk), :]
        q = q_scr[slot]
        b = b_scr[slot]
        c = c_scr[slot]
        b_rows = b_scr.at[slot]
        c_rows = c_scr.at[slot]
        vb = v.astype(BF16)
        b_last = b_rows[pl.ds(chunk - 1, 1), :]
        qd = (q * jnp.exp2(b)).astype(BF16)
        kd = jnp.exp2(b_last - c).astype(BF16)
        s_decay = jnp.exp2(b_last)

        slabs = []
        for r in range(sub):
            parts = []
            for i in range(n_sub):
                c_row = c_rows[pl.ds(i * sub + r, 1), :]
                sl = slice(i * sub, (i + 1) * sub)
                parts.append(q[sl] * jnp.exp2(jnp.minimum(b[sl] - c_row, 0.0)))
            slabs.append(_cat(parts, 0).astype(BF16))
        xcat = _cat([_cat([s[:, hs[h]] for s in slabs], 1) for h in range(hp)], 0)
        a_all = jnp.where(diag_mask_all, jnp.dot(xcat, wsel, preferred_element_type=F32), 0.0)

        if n_sub > 1:
            starts = [None] + [b_rows[pl.ds(i * sub - 1, 1), :] for i in range(1, n_sub)]
            ends = [b_rows[pl.ds(i * sub + sub - 1, 1), :] for i in range(n_sub)]
            qh, kh = [], []
            for i in range(n_sub):
                sl = slice(i * sub, (i + 1) * sub)
                qh.append(q[sl] if i == 0 else q[sl] * jnp.exp2(b[sl] - starts[i]))
                kh.append(jnp.exp2(ends[i] - c[sl]))
            khat = _cat(kh, 0).astype(BF16)
            qds = []
            for d in range(1, n_sub):
                parts = []
                for i in range(n_sub):
                    if i < d:
                        parts.append(jnp.zeros((sub, hp * HEAD_DIM), F32))
                    elif d == 1:
                        parts.append(qh[i])
                    else:
                        parts.append(qh[i] * jnp.exp2(starts[i] - ends[i - d]))
                qds.append(_cat(parts, 0).astype(BF16))
            qstack = _cat(qds, 0)

        outs = []
        for h in range(hp):
            st = st_ref[h]
            o = lax.dot_general(qd[:, hs[h]], st.astype(BF16), _NT, preferred_element_type=F32)
            st_ref[h] = st * s_decay[:, hs[h]] + lax.dot_general(vb[:, hs[h]], kd[:, hs[h]], _TN,
                                                                 preferred_element_type=F32)
            a = a_all[h * chunk:(h + 1) * chunk]
            if n_sub > 1:
                m = lax.dot_general(qstack[:, hs[h]], khat[:, hs[h]], _NT, preferred_element_type=F32)
                for d in range(1, n_sub):
                    a = a + jnp.where(blk_diff == d, m[(d - 1) * chunk:d * chunk], 0.0)
            o = o + jnp.dot(a.astype(BF16), vb[:, hs[h]], preferred_element_type=F32)
            ms = jnp.mean(o * o, axis=-1, keepdims=True)
            outs.append((o * lax.rsqrt(ms + NORM_EPS)) * nw[:, hs[h]])
        g = hg * _sigmoid(hg)
        o_ref[pl.ds(r0, chunk), :] = (_cat(outs, 1) * g).astype(o_ref.dtype)

    n_chunks = tb // chunk
    prepare(0, 0)
    if n_chunks == 1:
        main(0, 0)
    else:
        def pair(pp, carry):
            ci = 2 * pp
            prepare(ci + 1, 1)
            main(ci, 0)
            prepare(jnp.minimum(ci + 2, n_chunks - 1), 0)
            main(ci + 1, 1)
            return carry

        lax.fori_loop(0, n_chunks // 2, pair, 0)

    @pl.when(ti == pl.num_programs(2) - 1)
    def _():
        for h in range(hp):
            sout_ref[0, h] = st_ref[h].T


def _hgrn_consts(chunk, sub):
    t = np.arange(chunk)
    tri = (t[None, :] <= t[:, None]).astype(np.float32)
    rows = np.arange(sub * HEAD_DIM) // HEAD_DIM
    wsel = (rows[:, None] == (t[None, :] % sub)).astype(np.float32)
    return jnp.asarray(tri, BF16), jnp.asarray(wsel, BF16)


def _hgrn(proj, lb_logits, norm_w, state, batch, seq, heads, col0, hp, tb):
    chunk = min(HGRN_CHUNK, seq)
    sub = min(HGRN_SUB, chunk)
    tri, wsel = _hgrn_consts(chunk, sub)
    has_state = state is not None
    nt = seq // tb
    w = hp * HEAD_DIM

    def col_spec(g):
        return pl.BlockSpec((tb, w), lambda b, h, t: (b * nt + t, (col0 + g * heads) // hp + h))

    const = lambda b, h, t: (0, 0)
    in_specs = [col_spec(0), col_spec(1), col_spec(2), col_spec(3),
                pl.BlockSpec((lb_logits.shape[0], w), lambda b, h, t: (0, h)),
                pl.BlockSpec((1, w), const),
                pl.BlockSpec(tri.shape, const),
                pl.BlockSpec(wsel.shape, const)]
    args = [proj, proj, proj, proj, lb_logits, jnp.tile(norm_w, (1, hp)), tri, wsel]
    if has_state:
        in_specs.append(pl.BlockSpec((1, hp, HEAD_DIM, HEAD_DIM), lambda b, h, t: (b, h, 0, 0)))
        args.append(state)
    return pl.pallas_call(
        functools.partial(_hgrn_kernel, tb=tb, chunk=chunk, sub=sub, hp=hp, has_state=has_state),
        grid=(batch, heads // hp, nt),
        in_specs=in_specs,
        out_specs=[pl.BlockSpec((tb, w), lambda b, h, t: (b * nt + t, h)),
                   pl.BlockSpec((1, hp, HEAD_DIM, HEAD_DIM), lambda b, h, t: (b, h, 0, 0))],
        out_shape=[jax.ShapeDtypeStruct((batch * seq, heads * HEAD_DIM), BF16),
                   jax.ShapeDtypeStruct((batch, heads, HEAD_DIM, HEAD_DIM), F32)],
        scratch_shapes=[pltpu.VMEM((hp, HEAD_DIM, HEAD_DIM), F32),
                        pltpu.VMEM((2, chunk, w), F32),
                        pltpu.VMEM((2, chunk, w), F32),
                        pltpu.VMEM((2, chunk, w), F32)],
        compiler_params=_params(("arbitrary", "arbitrary", "arbitrary")),
        name="hgrn_state" if has_state else "hgrn",
    )(*args)


def _sb_blocks(qs, ks, vs, u, c, mask):
    tq = qs[0].shape[0]
    z = _cat([lax.dot_general(q, k, _NT, preferred_element_type=F32) for q, k in zip(qs, ks)], 0)
    nz = -z
    sp = jnp.maximum(nz, 0.0) + jnp.log2(1.0 + jnp.exp2(jnp.minimum(z, nz)))
    l = nz - sp
    if mask is not None:
        l = jnp.where(mask, l, 0.0)
    s = jnp.dot(l.astype(BF16), u, preferred_element_type=F32)
    p = jnp.exp2(s + c - sp)
    if mask is not None:
        p = jnp.where(mask, p, 0.0)
    p = p.astype(BF16)
    pv = _cat([jnp.dot(p[h * tq:(h + 1) * tq], vs[h], preferred_element_type=F32) for h in range(len(qs))], 0)
    return pv, c + (s[:, 0:1] + l[:, 0:1])


def _sb_suffix(tk):
    s = np.arange(tk)
    u = (s[:, None] > s[None, :]).astype(np.float32)
    return jnp.asarray(u, BF16)


def _sb_prompt_kernel(q_ref, k_ref, v_ref, u_ref, o_ref, ko_ref, vo_ref, acc_ref, c_ref, kb_ref, vb_ref,
                      lc_ref, sp_ref, l0_ref, *, tq, hp, scale):
    i = pl.program_id(2)
    hs = [slice(h * HEAD_DIM, (h + 1) * HEAD_DIM) for h in range(hp)]

    for h in range(hp):
        ko_ref[0, h] = k_ref[:, hs[h]]
        vo_ref[0, h] = v_ref[:, hs[h]]
    own = pl.ds(pl.multiple_of(i * tq, tq), tq)
    kb_ref[own, :] = k_ref[...].astype(BF16)
    vb_ref[own, :] = v_ref[...].astype(BF16)

    q = (q_ref[...] * scale).astype(BF16)
    usuf = u_ref[...]

    def scores(j, slot, diagonal):
        kb = kb_ref[pl.ds(pl.multiple_of(j * tq, tq), tq), :]
        z = _cat([lax.dot_general(q[:, hs[h]], kb[:, hs[h]], _NT, preferred_element_type=F32) for h in range(hp)], 0)
        nz = -z
        sp = jnp.maximum(nz, 0.0) + jnp.log2(1.0 + jnp.exp2(jnp.minimum(z, nz)))
        l = nz - sp
        if diagonal:
            row = lax.broadcasted_iota(jnp.int32, (tq, tq), 0)
            col = lax.broadcasted_iota(jnp.int32, (tq, tq), 1)
            mask = _cat([col < row] * hp, 0)
            l = jnp.where(mask, l, 0.0)
            sp = jnp.where(mask, sp, -NEG_BIG)
        lc_ref[slot] = l.astype(BF16)
        sp_ref[slot] = sp
        l0_ref[slot] = l[:, 0:1]

    def values(j, slot):
        vb = vb_ref[pl.ds(pl.multiple_of(j * tq, tq), tq), :]
        s = jnp.dot(lc_ref[slot], usuf, preferred_element_type=F32)
        c = c_ref[...]
        p = jnp.exp2(s + c - sp_ref[slot]).astype(BF16)
        acc_ref[...] += _cat([jnp.dot(p[h * tq:(h + 1) * tq], vb[:, hs[h]], preferred_element_type=F32)
                              for h in range(hp)], 0)
        c_ref[...] = c + (s[:, 0:1] + l0_ref[slot])

    acc_ref[...] = jnp.zeros_like(acc_ref)
    c_ref[...] = jnp.zeros_like(c_ref)
    scores(i, 0, True)

    def pair(pp, carry):
        j = i - 1 - 2 * pp
        scores(j, 1, False)
        values(j + 1, 0)
        scores(j - 1, 0, False)
        values(j, 1)
        return carry

    lax.fori_loop(0, i // 2, pair, 0)

    @pl.when(i % 2 == 1)
    def _():
        scores(0, 1, False)
        values(1, 0)
        values(0, 1)

    @pl.when(i % 2 == 0)
    def _():
        values(0, 0)
    o_ref[...] = _cat([acc_ref[pl.ds(h * tq, tq), :] for h in range(hp)], 1).astype(o_ref.dtype)


def _sb_prompt(proj, batch, seq, heads, col0, tq, hp):
    nq = seq // tq
    w = hp * HEAD_DIM
    usuf = _sb_suffix(tq)
    kv_shape = jax.ShapeDtypeStruct((batch, heads, seq, HEAD_DIM), F32)
    return pl.pallas_call(
        functools.partial(_sb_prompt_kernel, tq=tq, hp=hp, scale=HEAD_DIM ** -0.5 * LOG2E),
        grid=(batch, heads // hp, nq),
        in_specs=[pl.BlockSpec((tq, w), lambda b, h, i: (b * nq + i, col0 // hp + h)),
                  pl.BlockSpec((tq, w), lambda b, h, i: (b * nq + i, (col0 + heads) // hp + h)),
                  pl.BlockSpec((tq, w), lambda b, h, i: (b * nq + i, (col0 + 2 * heads) // hp + h)),
                  pl.BlockSpec(usuf.shape, lambda b, h, i: (0, 0))],
        out_specs=[pl.BlockSpec((tq, w), lambda b, h, i: (b * nq + i, h)),
                   pl.BlockSpec((1, hp, tq, HEAD_DIM), lambda b, h, i: (b, h, i, 0)),
                   pl.BlockSpec((1, hp, tq, HEAD_DIM), lambda b, h, i: (b, h, i, 0))],
        out_shape=[jax.ShapeDtypeStruct((batch * seq, heads * HEAD_DIM), BF16), kv_shape, kv_shape],
        scratch_shapes=[pltpu.VMEM((hp * tq, HEAD_DIM), F32), pltpu.VMEM((hp * tq, 1), F32),
                        pltpu.VMEM((seq, w), BF16), pltpu.VMEM((seq, w), BF16),
                        pltpu.VMEM((2, hp * tq, tq), BF16), pltpu.VMEM((2, hp * tq, tq), F32),
                        pltpu.VMEM((2, hp * tq, 1), F32)],
        compiler_params=_params(("arbitrary", "arbitrary", "arbitrary")),
        name="sb_prompt",
    )(proj, proj, proj, usuf)


def _sb_decode_kernel(q_ref, k_ref, v_ref, pk_ref, pv_ref, un_ref, up_ref, o_ref, acc_ref, c_ref,
                      *, seq, heads, tk, scale):
    j = pl.program_id(1)
    hs = [slice(h * HEAD_DIM, (h + 1) * HEAD_DIM) for h in range(heads)]
    q = (q_ref[...] * scale).astype(BF16)
    qs = [q[:, hs[h]] for h in range(heads)]

    @pl.when(j == 0)
    def _():
        pad = jnp.zeros((LANES - seq, heads * HEAD_DIM), F32)
        kn = jnp.concatenate([k_ref[...], pad], axis=0).astype(BF16)
        vn = jnp.concatenate([v_ref[...], pad], axis=0).astype(BF16)
        row = lax.broadcasted_iota(jnp.int32, (seq, LANES), 0)
        col = lax.broadcasted_iota(jnp.int32, (seq, LANES), 1)
        mask = _cat([col < row] * heads, 0)
        pv, c = _sb_blocks(qs, [kn[:, hs[h]] for h in range(heads)], [vn[:, hs[h]] for h in range(heads)],
                           un_ref[...], jnp.zeros((heads * seq, 1), F32), mask)
        acc_ref[...] = pv
        c_ref[...] = c

    usuf = up_ref[...]
    for sblk in reversed(range(pk_ref.shape[2] // tk)):
        ks = [pk_ref[0, h, pl.ds(sblk * tk, tk), :].astype(BF16) for h in range(heads)]
        vs = [pv_ref[0, h, pl.ds(sblk * tk, tk), :].astype(BF16) for h in range(heads)]
        pv, c = _sb_blocks(qs, ks, vs, usuf, c_ref[...], None)
        acc_ref[...] += pv
        c_ref[...] = c

    @pl.when(j == pl.num_programs(1) - 1)
    def _():
        o_ref[...] = _cat([acc_ref[pl.ds(h * seq, seq), :] for h in range(heads)], 1).astype(o_ref.dtype)


def _sb_decode(proj, past_k, past_v, batch, seq, heads, col0, tk, tkb):
    past = past_k.shape[2]
    nblk = past // tkb
    w = heads * HEAD_DIM
    un, up = _sb_suffix(LANES), _sb_suffix(tk)
    kv_spec = pl.BlockSpec((1, heads, tkb, HEAD_DIM), lambda b, j: (b, 0, nblk - 1 - j, 0))
    return pl.pallas_call(
        functools.partial(_sb_decode_kernel, seq=seq, heads=heads, tk=tk, scale=HEAD_DIM ** -0.5 * LOG2E),
        grid=(batch, nblk),
        in_specs=[pl.BlockSpec((seq, w), lambda b, j: (b, col0 // heads)),
                  pl.BlockSpec((seq, w), lambda b, j: (b, col0 // heads + 1)),
                  pl.BlockSpec((seq, w), lambda b, j: (b, col0 // heads + 2)),
                  kv_spec, kv_spec,
                  pl.BlockSpec(un.shape, lambda b, j: (0, 0)),
                  pl.BlockSpec(up.shape, lambda b, j: (0, 0))],
        out_specs=pl.BlockSpec((seq, w), lambda b, j: (b, 0)),
        out_shape=jax.ShapeDtypeStruct((batch * seq, w), BF16),
        scratch_shapes=[pltpu.VMEM((heads * seq, HEAD_DIM), F32), pltpu.VMEM((heads * seq, 1), F32)],
        compiler_params=_params(("arbitrary", "arbitrary")),
        name="sb_decode",
    )(proj, proj, proj, past_k, past_v, un, up)


def _route(logits):
    lane = lax.broadcasted_iota(jnp.int32, logits.shape, 1).astype(F32)
    first = lambda hit: jnp.min(jnp.where(hit, lane, float(LANES)), axis=-1, keepdims=True)
    is_g = lane < N_GROUPS
    gl = jnp.where(is_g, logits, NEG_BIG)
    gmax = jnp.max(gl, axis=-1, keepdims=True)
    gidx = first(gl == gmax)
    gsum = jnp.sum(jnp.where(is_g, jnp.exp(gl - gmax), 0.0), axis=-1, keepdims=True)
    gw = 1.0 / gsum
    lo = N_GROUPS + EXPERTS_PER_GROUP * gidx
    in_g = (lane >= lo) & (lane < lo + EXPERTS_PER_GROUP)
    el = jnp.where(in_g, logits, NEG_BIG)
    m1 = jnp.max(el, axis=-1, keepdims=True)
    i1 = first(in_g & (el == m1))
    rest = in_g & (lane != i1)
    el2 = jnp.where(rest, logits, NEG_BIG)
    m2 = jnp.max(el2, axis=-1, keepdims=True)
    i2 = first(rest & (el2 == m2))
    t = jnp.exp(m2 - m1)
    w0 = gw / (1.0 + t)
    w1 = gw * t / (1.0 + t)
    e0 = i1 - N_GROUPS
    e1 = i2 - N_GROUPS
    return jnp.where(lane == 0, e0, jnp.where(lane == 1, e1, jnp.where(lane == 2, w0, jnp.where(lane == 3, w1, 0.0))))


def _outproj_kernel(x_ref, a_ref, b_ref, wa_ref, wb_ref, nw_ref, rw_ref, rb_ref, *rest):
    x2_ref, xn_ref, r_ref = rest[-3:]
    wh = rw_ref[0]
    half = min(x_ref.shape[0], 256)
    for h0 in range(0, x_ref.shape[0], half):
        rows = pl.ds(h0, half)
        acc = jnp.dot(a_ref[rows, :], wa_ref[...], preferred_element_type=F32)
        acc = acc + jnp.dot(b_ref[rows, :], wb_ref[...], preferred_element_type=F32)
        x2 = x_ref[rows, :] + acc
        x2_ref[rows, :] = x2
        ms = jnp.mean(x2 * x2, axis=-1, keepdims=True)
        xn = (x2 * lax.rsqrt(ms + NORM_EPS)) * nw_ref[...]
        xn_ref[rows, :] = xn
        xh = xn.astype(BF16)
        xl = (xn - xh.astype(F32)).astype(BF16)
        logits = (jnp.dot(xh, wh, preferred_element_type=F32) + jnp.dot(xh, rw_ref[1], preferred_element_type=F32)
                  + jnp.dot(xl, wh, preferred_element_type=F32)) + rb_ref[...]
        r_ref[rows, :] = _route(logits)


def _outproj(x, oa, ob, wa, wb, nw, rw, rb, tm, n_total, row0, prev):
    n, d = x.shape
    blk0 = row0 // tm
    row = lambda i: (i, 0)
    orow = lambda i: (blk0 + i, 0)
    const = lambda i: (0, 0)
    in_specs = [pl.BlockSpec((tm, d), row),
                pl.BlockSpec((tm, oa.shape[1]), row),
                pl.BlockSpec((tm, ob.shape[1]), row),
                pl.BlockSpec(wa.shape, const),
                pl.BlockSpec(wb.shape, const),
                pl.BlockSpec((1, d), const),
                pl.BlockSpec(rw.shape, lambda i: (0, 0, 0)),
                pl.BlockSpec((1, LANES), const)]
    args = [x, oa, ob, wa, wb, nw, rw, rb]
    aliases = {}
    if prev is not None:
        in_specs += [pl.BlockSpec(memory_space=pl.ANY)] * 3
        aliases = {len(args) + t: t for t in range(3)}
        args += list(prev)
    return pl.pallas_call(
        _outproj_kernel,
        grid=(n // tm,),
        in_specs=in_specs,
        out_specs=[pl.BlockSpec((tm, d), orow), pl.BlockSpec((tm, d), orow), pl.BlockSpec((tm, LANES), orow)],
        out_shape=[jax.ShapeDtypeStruct((n_total, d), F32), jax.ShapeDtypeStruct((n_total, d), F32),
                   jax.ShapeDtypeStruct((n_total, LANES), F32)],
        input_output_aliases=aliases,
        compiler_params=_params(("arbitrary",)),
        name="outproj",
    )(*args)


def _weight_copies(e, slot, w_hbm, w_stage, sem):
    return [pltpu.make_async_copy(w_hbm[t].at[e], w_stage[t].at[slot], sem.at[slot, t]) for t in range(3)]


def _row_in(src_hbm, row, dst, r, sem):
    return pltpu.make_async_copy(src_hbm.at[pl.ds(row, 1), :], dst.at[pl.ds(r, 1), :], sem)


def _experts_kernel(te_ref, nu_ref, nx_ref, ig_ref, pos_ref, ps_ref, pl_ref, x_hbm, wg_hbm, wu_hbm, wd_hbm, o_ref,
                    sg, su, sd, g_scr, u_scr, d_scr, xbuf, row_tok, sem, gsem, *, tm):
    m = pl.program_id(0)
    w_hbm, w_stage = (wg_hbm, wu_hbm, wd_hbm), (sg, su, sd)

    def gather(tile, s):
        def go(r, carry):
            _row_in(x_hbm, row_tok[tile * tm + r], xbuf.at[s], r, gsem.at[s]).start()
            return carry
        lax.fori_loop(0, tm, go, 0, unroll=8)

    @pl.when(m == 0)
    def _():
        last = row_tok.shape[0] - 1

        def pad(e, carry):
            def eight(k, c):
                for t in range(8):
                    row_tok[jnp.minimum(ps_ref[e] + 8 * k + t, last)] = 0
                return c
            return lax.fori_loop(0, lax.shift_right_logical(pl_ref[e] + 7, 3), eight, carry)
        lax.fori_loop(0, N_EXPERTS, pad, 0)

        def invert(a, carry):
            row_tok[pos_ref[a]] = lax.shift_right_logical(a, 1)
            return carry
        lax.fori_loop(0, pos_ref.shape[0], invert, 0, unroll=8)
        gather(0, 0)

    @pl.when(m < nu_ref[0])
    def _():
        e = te_ref[m]
        new_expert = (m == 0) | (e != te_ref[jnp.maximum(m - 1, 0)])
        slot = ig_ref[m] % 2
        xs = m % 2

        @pl.when(m + 1 < nu_ref[0])
        def _():
            gather(m + 1, 1 - xs)

        @pl.when(m == 0)
        def _():
            for cp in _weight_copies(e, slot, w_hbm, w_stage, sem):
                cp.start()

        @pl.when(new_expert)
        def _():
            for cp in _weight_copies(e, slot, w_hbm, w_stage, sem):
                cp.wait()

            @pl.when(nx_ref[m] != e)
            def _():
                for cp in _weight_copies(nx_ref[m], 1 - slot, w_hbm, w_stage, sem):
                    cp.start()

            g_scr[...] = sg[slot].astype(BF16)
            u_scr[...] = su[slot].astype(BF16)
            d_scr[...] = sd[slot].astype(BF16)

        def landed(r, carry):
            _row_in(x_hbm, 0, xbuf.at[xs], r, gsem.at[xs]).wait()
            return carry
        lax.fori_loop(0, tm, landed, 0, unroll=8)
        x = xbuf[xs].astype(BF16)
        g = jnp.dot(x, g_scr[...], preferred_element_type=F32)
        u = jnp.dot(x, u_scr[...], preferred_element_type=F32)
        hid = ((g * _sigmoid(g)) * u).astype(BF16)
        o_ref[...] = jnp.dot(hid, d_scr[...], preferred_element_type=F32)


def _experts(tile_expert, n_used, tile_next, tile_group, pos, pad_start, pad_len, x, wg, wu, wd, tm):
    d, ff = wg.shape[1], wg.shape[2]
    tiles = tile_expert.shape[0]
    tile = lambda m, te, nu, *_: (jnp.minimum(m, jnp.maximum(nu[0] - 1, 0)), 0)
    hbm = pl.BlockSpec(memory_space=pl.ANY)
    return pl.pallas_call(
        functools.partial(_experts_kernel, tm=tm),
        grid_spec=pltpu.PrefetchScalarGridSpec(
            num_scalar_prefetch=7,
            grid=(tiles,),
            in_specs=[hbm, hbm, hbm, hbm],
            out_specs=pl.BlockSpec((tm, d), tile),
            scratch_shapes=[pltpu.VMEM((2, d, ff), F32), pltpu.VMEM((2, d, ff), F32), pltpu.VMEM((2, ff, d), F32),
                            pltpu.VMEM((d, ff), BF16), pltpu.VMEM((d, ff), BF16), pltpu.VMEM((ff, d), BF16),
                            pltpu.VMEM((2, tm, d), F32), pltpu.SMEM((tiles * tm,), jnp.int32),
                            pltpu.SemaphoreType.DMA((2, 3)), pltpu.SemaphoreType.DMA((2,))]),
        out_shape=jax.ShapeDtypeStruct((tiles * tm, d), F32),
        compiler_params=_params(("arbitrary",)),
        name="experts",
    )(tile_expert, n_used, tile_next, tile_group, pos, pad_start, pad_len, x, wg, wu, wd)


def _combine_kernel(pos_ref, nxt_ref, x_ref, r_ref, fw_ref, ys_hbm, o_ref, buf, sem, *, tm, ntiles):
    m = pl.program_id(0)
    slot = m % 2

    def issue(tbl, s):
        def go(r, carry):
            _row_in(ys_hbm, tbl[0, 0, 2 * r], buf.at[s, 0], r, sem.at[s]).start()
            _row_in(ys_hbm, tbl[0, 0, 2 * r + 1], buf.at[s, 1], r, sem.at[s]).start()
            return carry
        lax.fori_loop(0, tm, go, 0, unroll=8)

    @pl.when(m == 0)
    def _():
        issue(pos_ref, 0)

    @pl.when(m + 1 < ntiles)
    def _():
        issue(nxt_ref, 1 - slot)

    def drain(r, carry):
        _row_in(ys_hbm, 0, buf.at[slot, 0], r, sem.at[slot]).wait()
        _row_in(ys_hbm, 0, buf.at[slot, 1], r, sem.at[slot]).wait()
        return carry
    lax.fori_loop(0, tm, drain, 0, unroll=8)
    rt = r_ref[...]
    x = x_ref[...] + (buf[slot, 0] * rt[:, 2:3] + buf[slot, 1] * rt[:, 3:4])
    ms = jnp.mean(x * x, axis=-1, keepdims=True)
    o_ref[...] = (x * lax.rsqrt(ms + NORM_EPS)) * fw_ref[...]


def _combine(pos, x2, r, fw, ys, tm, row0, n):
    d = x2.shape[1]
    blk0 = row0 // tm
    ntiles = n // tm
    smem = functools.partial(pl.BlockSpec, memory_space=pltpu.SMEM)
    return pl.pallas_call(
        functools.partial(_combine_kernel, tm=tm, ntiles=ntiles),
        grid=(ntiles,),
        in_specs=[smem((1, 1, 2 * tm), lambda m: (blk0 + m, 0, 0)),
                  smem((1, 1, 2 * tm), lambda m: (blk0 + jnp.minimum(m + 1, ntiles - 1), 0, 0)),
                  pl.BlockSpec((tm, d), lambda m: (blk0 + m, 0)),
                  pl.BlockSpec((tm, LANES), lambda m: (blk0 + m, 0)),
                  pl.BlockSpec((1, d), lambda m: (0, 0)),
                  pl.BlockSpec(memory_space=pl.ANY)],
        out_specs=pl.BlockSpec((tm, d), lambda m: (m, 0)),
        out_shape=jax.ShapeDtypeStruct((n, d), F32),
        scratch_shapes=[pltpu.VMEM((2, 2, tm, d), F32), pltpu.SemaphoreType.DMA((2,))],
        compiler_params=_params(("arbitrary",)),
        name="combine",
    )(pos, pos, x2, r, fw, ys)


def _dispatch_tables(r, tm):
    n = r.shape[0]
    tiles = (2 * n) // tm + N_EXPERTS
    ids = jnp.arange(N_EXPERTS, dtype=jnp.int32)
    e = r[:, 0:2].astype(jnp.int32).reshape(-1)
    onehot = (e[:, None] == ids[None, :]).astype(jnp.int32)
    incl = jnp.cumsum(onehot, axis=0)
    counts = incl[-1]
    etiles = (counts + tm - 1) // tm
    tile_end = jnp.cumsum(etiles)
    tile_start = tile_end - etiles
    n_used = tile_end[-1]
    pos = jnp.sum(onehot * (incl - 1 + (tile_start * tm)[None, :]), axis=1)
    m = jnp.minimum(jnp.arange(tiles, dtype=jnp.int32), n_used - 1)
    te = jnp.sum((tile_end[None, :] <= m[:, None]).astype(jnp.int32), axis=1)
    sel = (te[:, None] == ids[None, :]).astype(jnp.int32)
    pick = lambda tbl: jnp.sum(sel * tbl[None, :], axis=1)
    used = counts > 0
    later = used[None, :] & (ids[None, :] > ids[:, None])
    next_used = jnp.min(jnp.where(later, ids[None, :], N_EXPERTS), axis=1)
    next_used = jnp.where(next_used == N_EXPERTS, ids, next_used)
    group = jnp.cumsum(used.astype(jnp.int32)) - 1
    i32 = lambda a: a.astype(jnp.int32)
    return (i32(pos), i32(te), i32(n_used).reshape(1), i32(pick(next_used)), i32(pick(group)),
            i32(tile_start * tm + counts), i32(etiles * tm - counts))


def kernel(x_prompt, x_sample, cache_sb_k, cache_sb_v, state_hgrn, w_in, hg_lb_logits, hg_norm_w, w_out, norm1_w,
           norm2_w, router_group_w, router_group_b, router_expert_w, router_expert_b, expert_w_gate, expert_w_up,
           expert_w_down, final_norm_w):
    bp, tp, d = x_prompt.shape
    bs, ts, _ = x_sample.shape
    np_, ns = bp * tp, bs * ts
    n = np_ + ns
    heads = w_in.shape[2] // (7 * HEAD_DIM)
    hw = heads * HEAD_DIM
    tm_tok = 256
    tm_moe = 256

    w_in_b = w_in[0].astype(BF16)
    w_out_b = w_out[0].astype(BF16)
    n1 = norm1_w[0].reshape(1, d)
    n2 = norm2_w[0].reshape(1, d)
    hnw = hg_norm_w[0].reshape(1, HEAD_DIM)
    fw = final_norm_w.reshape(1, d)

    xp = x_prompt.reshape(np_, d)
    xs = x_sample.reshape(ns, d)
    proj_p = _inproj(xp, n1, w_in_b, 256, w_in_b.shape[1] // 2)
    proj_s = _inproj(xs, n1, w_in_b, ns, w_in_b.shape[1] // 2)

    ohg_p, st_p = _hgrn(proj_p, hg_lb_logits, hnw, None, bp, tp, heads, 0, 4, 512)
    ohg_s, st_s = _hgrn(proj_s, hg_lb_logits, hnw, state_hgrn[0], bs, ts, heads, 0, heads, ts)
    osb_p, k_p, v_p = _sb_prompt(proj_p, bp, tp, heads, 4 * heads, 256, 8)
    osb_s = _sb_decode(proj_s, cache_sb_k[0], cache_sb_v[0], bs, ts, heads, 4 * heads, 512, 1024)

    rw = jnp.concatenate([router_group_w[0], router_expert_w[0]], axis=1)
    rw = jnp.pad(rw, ((0, 0), (0, LANES - rw.shape[1])))
    rwh = rw.astype(BF16)
    rw2 = jnp.stack([rwh, (rw - rwh.astype(F32)).astype(BF16)])
    rb = jnp.concatenate([router_group_b[0], router_expert_b[0]])
    rb = jnp.pad(rb, (0, LANES - rb.shape[0])).reshape(1, LANES)

    wa, wb = w_out_b[:hw], w_out_b[hw:]
    outs = _outproj(xp, ohg_p, osb_p, wa, wb, n2, rw2, rb, 2 * tm_tok, n, 0, None)
    x2, xn2, r = _outproj(xs, ohg_s, osb_s, wa, wb, n2, rw2, rb, tm_tok, n, np_, outs)

    pos, tile_expert, n_used, tile_next, tile_group, pad_start, pad_len = _dispatch_tables(r, tm_moe)
    ys = _experts(tile_expert, n_used, tile_next, tile_group, pos, pad_start, pad_len, xn2,
                  expert_w_gate[0], expert_w_up[0], expert_w_down[0], tm_moe)
    pos = pos.reshape(n // tm_tok, 1, 2 * tm_tok)
    y_p = _combine(pos, x2, r, fw, ys, tm_tok, 0, np_)
    y_s = _combine(pos, x2, r, fw, ys, tm_tok, np_, ns)

    heads_s = lambda a: a.reshape(bs, ts, heads, HEAD_DIM).transpose(0, 2, 1, 3)[None]
    k_s = heads_s(proj_s[:, 5 * hw:6 * hw])
    v_s = heads_s(proj_s[:, 6 * hw:7 * hw])
    return (y_p.reshape(bp, tp, d), y_s.reshape(bs, ts, d), k_p[None], v_p[None], st_p[None],
            k_s, v_s, st_s[None])
```

```python
import functools

import jax
import jax.numpy as jnp
import numpy as np
from jax import lax
from jax.experimental import pallas as pl
from jax.experimental.pallas import tpu as pltpu

F32 = jnp.float32
BF16 = jnp.bfloat16
NORM_EPS = 1e-6
LANES = 128
HEAD_DIM = 128
HGRN_CHUNK = 64
HGRN_SUB = 16
N_GROUPS = 4
EXPERTS_PER_GROUP = 8
N_EXPERTS = N_GROUPS * EXPERTS_PER_GROUP
VMEM_LIMIT = 48 * 1024 * 1024
NEG_BIG = -1e30
LOG2E = 1.4426950408889634

_NT = (((1,), (1,)), ((), ()))
_TN = (((0,), (0,)), ((), ()))


def _params(sem):
    return pltpu.CompilerParams(dimension_semantics=sem, vmem_limit_bytes=VMEM_LIMIT)


def _sigmoid(x):
    return 1.0 / (1.0 + jnp.exp(-x))


def _split3(x):
    h = x.astype(BF16)
    r = x - h.astype(F32)
    m = r.astype(BF16)
    l = (r - m.astype(F32)).astype(BF16)
    return h, m, l


def _cat(parts, axis):
    return parts[0] if len(parts) == 1 else jnp.concatenate(parts, axis=axis)


def _inproj_kernel(x_ref, nw_ref, w_ref, o_ref):
    x = x_ref[...]
    ms = jnp.mean(x * x, axis=-1, keepdims=True)
    xn = ((x * lax.rsqrt(ms + NORM_EPS)) * nw_ref[...]).astype(BF16)
    o_ref[...] = jnp.dot(xn, w_ref[...], preferred_element_type=F32)


def _inproj(x, nw, w, tm, tn):
    n, d = x.shape
    c = w.shape[1]
    return pl.pallas_call(
        _inproj_kernel,
        grid=(c // tn, n // tm),
        in_specs=[pl.BlockSpec((tm, d), lambda j, i: (i, 0)),
                  pl.BlockSpec((1, d), lambda j, i: (0, 0)),
                  pl.BlockSpec((d, tn), lambda j, i: (0, j))],
        out_specs=pl.BlockSpec((tm, tn), lambda j, i: (i, j)),
        out_shape=jax.ShapeDtypeStruct((n, c), F32),
        compiler_params=_params(("arbitrary", "arbitrary")),
        name="inproj",
    )(x, nw, w)


def _hgrn_kernel(*refs, tb, chunk, sub, hp, has_state):
    if has_state:
        (hq_ref, hf_ref, hi_ref, hg_ref, lbl_ref, nw_ref, tri_ref, wsel_ref, s0_ref,
         o_ref, sout_ref, st_ref, q_scr, b_scr, c_scr) = refs
    else:
        (hq_ref, hf_ref, hi_ref, hg_ref, lbl_ref, nw_ref, tri_ref, wsel_ref,
         o_ref, sout_ref, st_ref, q_scr, b_scr, c_scr) = refs
    n_sub = chunk // sub
    ti = pl.program_id(2)
    hs = [slice(h * HEAD_DIM, (h + 1) * HEAD_DIM) for h in range(hp)]

    lg = lbl_ref[...]
    lg = jnp.exp(lg - jnp.max(lg, axis=0, keepdims=True))
    lb = lg[0:1, :] / jnp.sum(lg, axis=0, keepdims=True)

    @pl.when(ti == 0)
    def _():
        for h in range(hp):
            st_ref[h] = s0_ref[0, h].T if has_state else jnp.zeros((HEAD_DIM, HEAD_DIM), F32)

    row = lax.broadcasted_iota(jnp.int32, (chunk, chunk), 0)
    col = lax.broadcasted_iota(jnp.int32, (chunk, chunk), 1)
    blk_diff = row // sub - col // sub
    diag_mask = (blk_diff == 0) & (col <= row)
    diag_mask_all = _cat([diag_mask] * hp, 0)
    tri = tri_ref[...]
    wsel = wsel_ref[...]
    nw = nw_ref[...]

    def prepare(ci, slot):
        r0 = pl.multiple_of(ci * chunk, chunk)
        hq = hq_ref[pl.ds(r0, chunk), :]
        hf = hf_ref[pl.ds(r0, chunk), :]
        f = lb + (1.0 - lb) * _sigmoid(hf)
        k = 1.0 - f
        l1, l2, l3 = _split3(jnp.log(f) * LOG2E)
        b = (jnp.dot(tri, l1, preferred_element_type=F32) + jnp.dot(tri, l2, preferred_element_type=F32)
             + jnp.dot(tri, l3, preferred_element_type=F32))
        q_scr[slot] = hq * _sigmoid(hq)
        b_scr[slot] = b
        c_scr[slot] = b - jnp.log(k) * LOG2E

    def main(ci, slot):
        r0 = pl.multiple_of(ci * chunk, chunk)
        v = hi_ref[pl.ds(r0, chunk), :]
        hg = hg_ref[pl.ds(r0, chunk), :]
        q = q_scr[slot]
        b = b_scr[slot]
        c = c_scr[slot]
        b_rows = b_scr.at[slot]
        c_rows = c_scr.at[slot]
        vb = v.astype(BF16)
        b_last = b_rows[pl.ds(chunk - 1, 1), :]
        qd = (q * jnp.exp2(b)).astype(BF16)
        kd = jnp.exp2(b_last - c).astype(BF16)
        s_decay = jnp.exp2(b_last)

        slabs = []
        for r in range(sub):
            parts = []
            for i in range(n_sub):
                c_row = c_rows[pl.ds(i * sub + r, 1), :]
                sl = slice(i * sub, (i + 1) * sub)
                parts.append(q[sl] * jnp.exp2(jnp.minimum(b[sl] - c_row, 0.0)))
            slabs.append(_cat(parts, 0).astype(BF16))
        xcat = _cat([_cat([s[:, hs[h]] for s in slabs], 1) for h in range(hp)], 0)
        a_all = jnp.where(diag_mask_all, jnp.dot(xcat, wsel, preferred_element_type=F32), 0.0)

        if n_sub > 1:
            starts = [None] + [b_rows[pl.ds(i * sub - 1, 1), :] for i in range(1, n_sub)]
            ends = [b_rows[pl.ds(i * sub + sub - 1, 1), :] for i in range(n_sub)]
            qh, kh = [], []
            for i in range(n_sub):
                sl = slice(i * sub, (i + 1) * sub)
                qh.append(q[sl] if i == 0 else q[sl] * jnp.exp2(b[sl] - starts[i]))
                kh.append(jnp.exp2(ends[i] - c[sl]))
            khat = _cat(kh, 0).astype(BF16)
            qds = []
            for d in range(1, n_sub):
                parts = []
                for i in range(n_sub):
                    if i < d:
                        parts.append(jnp.zeros((sub, hp * HEAD_DIM), F32))
                    elif d == 1:
                        parts.append(qh[i])
                    else:
                        parts.append(qh[i] * jnp.exp2(starts[i] - ends[i - d]))
                qds.append(_cat(parts, 0).astype(BF16))
            qstack = _cat(qds, 0)

        outs = []
        for h in range(hp):
            st = st_ref[h]
            o = lax.dot_general(qd[:, hs[h]], st.astype(BF16), _NT, preferred_element_type=F32)
            st_ref[h] = st * s_decay[:, hs[h]] + lax.dot_general(vb[:, hs[h]], kd[:, hs[h]], _TN,
                                                                 preferred_element_type=F32)
            a = a_all[h * chunk:(h + 1) * chunk]
            if n_sub > 1:
                m = lax.dot_general(qstack[:, hs[h]], khat[:, hs[h]], _NT, preferred_element_type=F32)
                for d in range(1, n_sub):
                    a = a + jnp.where(blk_diff == d, m[(d - 1) * chunk:d * chunk], 0.0)
            o = o + jnp.dot(a.astype(BF16), vb[:, hs[h]], preferred_element_type=F32)
            ms = jnp.mean(o * o, axis=-1, keepdims=True)
            outs.append((o * lax.rsqrt(ms + NORM_EPS)) * nw[:, hs[h]])
        g = hg * _sigmoid(hg)
        o_ref[pl.ds(r0, chunk), :] = (_cat(outs, 1) * g).astype(o_ref.dtype)

    n_chunks = tb // chunk
    prepare(0, 0)
    if n_chunks == 1:
        main(0, 0)
    else:
        def pair(pp, carry):
            ci = 2 * pp
            prepare(ci + 1, 1)
            main(ci, 0)
            prepare(jnp.minimum(ci + 2, n_chunks - 1), 0)
            main(ci + 1, 1)
            return carry

        lax.fori_loop(0, n_chunks // 2, pair, 0)

    @pl.when(ti == pl.num_programs(2) - 1)
    def _():
        for h in range(hp):
            sout_ref[0, h] = st_ref[h].T


def _hgrn_consts(chunk, sub):
    t = np.arange(chunk)
    tri = (t[None, :] <= t[:, None]).astype(np.float32)
    rows = np.arange(sub * HEAD_DIM) // HEAD_DIM
    wsel = (rows[:, None] == (t[None, :] % sub)).astype(np.float32)
    return jnp.asarray(tri, BF16), jnp.asarray(wsel, BF16)


def _hgrn(proj, lb_logits, norm_w, state, batch, seq, heads, col0, hp, tb):
    chunk = min(HGRN_CHUNK, seq)
    sub = min(HGRN_SUB, chunk)
    tri, wsel = _hgrn_consts(chunk, sub)
    has_state = state is not None
    nt = seq // tb
    w = hp * HEAD_DIM

    def col_spec(g):
        return pl.BlockSpec((tb, w), lambda b, h, t: (b * nt + t, (col0 + g * heads) // hp + h))

    const = lambda b, h, t: (0, 0)
    in_specs = [col_spec(0), col_spec(1), col_spec(2), col_spec(3),
                pl.BlockSpec((lb_logits.shape[0], w), lambda b, h, t: (0, h)),
                pl.BlockSpec((1, w), const),
                pl.BlockSpec(tri.shape, const),
                pl.BlockSpec(wsel.shape, const)]
    args = [proj, proj, proj, proj, lb_logits, jnp.tile(norm_w, (1, hp)), tri, wsel]
    if has_state:
        in_specs.append(pl.BlockSpec((1, hp, HEAD_DIM, HEAD_DIM), lambda b, h, t: (b, h, 0, 0)))
        args.append(state)
    return pl.pallas_call(
        functools.partial(_hgrn_kernel, tb=tb, chunk=chunk, sub=sub, hp=hp, has_state=has_state),
        grid=(batch, heads // hp, nt),
        in_specs=in_specs,
        out_specs=[pl.BlockSpec((tb, w), lambda b, h, t: (b * nt + t, h)),
                   pl.BlockSpec((1, hp, HEAD_DIM, HEAD_DIM), lambda b, h, t: (b, h, 0, 0))],
        out_shape=[jax.ShapeDtypeStruct((batch * seq, heads * HEAD_DIM), BF16),
                   jax.ShapeDtypeStruct((batch, heads, HEAD_DIM, HEAD_DIM), F32)],
        scratch_shapes=[pltpu.VMEM((hp, HEAD_DIM, HEAD_DIM), F32),
                        pltpu.VMEM((2, chunk, w), F32),
                        pltpu.VMEM((2, chunk, w), F32),
                        pltpu.VMEM((2, chunk, w), F32)],
        compiler_params=_params(("arbitrary", "arbitrary", "arbitrary")),
        name="hgrn_state" if has_state else "hgrn",
    )(*args)


def _sb_blocks(qs, ks, vs, u, c, mask):
    tq = qs[0].shape[0]
    z = _cat([lax.dot_general(q, k, _NT, preferred_element_type=F32) for q, k in zip(qs, ks)], 0)
    nz = -z
    sp = jnp.maximum(nz, 0.0) + jnp.log2(1.0 + jnp.exp2(jnp.minimum(z, nz)))
    l = nz - sp
    if mask is not None:
        l = jnp.where(mask, l, 0.0)
    s = jnp.dot(l.astype(BF16), u, preferred_element_type=F32)
    p = jnp.exp2(s + c - sp)
    if mask is not None:
        p = jnp.where(mask, p, 0.0)
    p = p.astype(BF16)
    pv = _cat([jnp.dot(p[h * tq:(h + 1) * tq], vs[h], preferred_element_type=F32) for h in range(len(qs))], 0)
    return pv, c + (s[:, 0:1] + l[:, 0:1])


def _sb_suffix(tk):
    s = np.arange(tk)
    u = (s[:, None] > s[None, :]).astype(np.float32)
    return jnp.asarray(u, BF16)


def _sb_prompt_kernel(q_ref, k_ref, v_ref, u_ref, o_ref, ko_ref, vo_ref, acc_ref, c_ref, kb_ref, vb_ref,
                      lc_ref, sp_ref, l0_ref, *, tq, hp, scale):
    i = pl.program_id(2)
    hs = [slice(h * HEAD_DIM, (h + 1) * HEAD_DIM) for h in range(hp)]

    for h in range(hp):
        ko_ref[0, h] = k_ref[:, hs[h]]
        vo_ref[0, h] = v_ref[:, hs[h]]
    own = pl.ds(pl.multiple_of(i * tq, tq), tq)
    kb_ref[own, :] = k_ref[...].astype(BF16)
    vb_ref[own, :] = v_ref[...].astype(BF16)

    q = (q_ref[...] * scale).astype(BF16)
    usuf = u_ref[...]

    def scores(j, slot, diagonal):
        kb = kb_ref[pl.ds(pl.multiple_of(j * tq, tq), tq), :]
        z = _cat([lax.dot_general(q[:, hs[h]], kb[:, hs[h]], _NT, preferred_element_type=F32) for h in range(hp)], 0)
        nz = -z
        sp = jnp.maximum(nz, 0.0) + jnp.log2(1.0 + jnp.exp2(jnp.minimum(z, nz)))
        l = nz - sp
        if diagonal:
            row = lax.broadcasted_iota(jnp.int32, (tq, tq), 0)
            col = lax.broadcasted_iota(jnp.int32, (tq, tq), 1)
            mask = _cat([col < row] * hp, 0)
            l = jnp.where(mask, l, 0.0)
            sp = jnp.where(mask, sp, -NEG_BIG)
        lc_ref[slot] = l.astype(BF16)
        sp_ref[slot] = sp
        l0_ref[slot] = l[:, 0:1]

    def values(j, slot):
        vb = vb_ref[pl.ds(pl.multiple_of(j * tq, tq), tq), :]
        s = jnp.dot(lc_ref[slot], usuf, preferred_element_type=F32)
        c = c_ref[...]
        p = jnp.exp2(s + c - sp_ref[slot]).astype(BF16)
        acc_ref[...] += _cat([jnp.dot(p[h * tq:(h + 1) * tq], vb[:, hs[h]], preferred_element_type=F32)
                              for h in range(hp)], 0)
        c_ref[...] = c + (s[:, 0:1] + l0_ref[slot])

    acc_ref[...] = jnp.zeros_like(acc_ref)
    c_ref[...] = jnp.zeros_like(c_ref)
    scores(i, 0, True)

    def pair(pp, carry):
        j = i - 1 - 2 * pp
        scores(j, 1, False)
        values(j + 1, 0)
        scores(j - 1, 0, False)
        values(j, 1)
        return carry

    lax.fori_loop(0, i // 2, pair, 0)

    @pl.when(i % 2 == 1)
    def _():
        scores(0, 1, False)
        values(1, 0)
        values(0, 1)

    @pl.when(i % 2 == 0)
    def _():
        values(0, 0)
    o_ref[...] = _cat([acc_ref[pl.ds(h * tq, tq), :] for h in range(hp)], 1).astype(o_ref.dtype)


def _sb_prompt(proj, batch, seq, heads, col0, tq, hp):
    nq = seq // tq
    w = hp * HEAD_DIM
    usuf = _sb_suffix(tq)
    kv_shape = jax.ShapeDtypeStruct((batch, heads, seq, HEAD_DIM), F32)
    return pl.pallas_call(
        functools.partial(_sb_prompt_kernel, tq=tq, hp=hp, scale=HEAD_DIM ** -0.5 * LOG2E),
        grid=(batch, heads // hp, nq),
        in_specs=[pl.BlockSpec((tq, w), lambda b, h, i: (b * nq + i, col0 // hp + h)),
                  pl.BlockSpec((tq, w), lambda b, h, i: (b * nq + i, (col0 + heads) // hp + h)),
                  pl.BlockSpec((tq, w), lambda b, h, i: (b * nq + i, (col0 + 2 * heads) // hp + h)),
                  pl.BlockSpec(usuf.shape, lambda b, h, i: (0, 0))],
        out_specs=[pl.BlockSpec((tq, w), lambda b, h, i: (b * nq + i, h)),
                   pl.BlockSpec((1, hp, tq, HEAD_DIM), lambda b, h, i: (b, h, i, 0)),
                   pl.BlockSpec((1, hp, tq, HEAD_DIM), lambda b, h, i: (b, h, i, 0))],
        out_shape=[jax.ShapeDtypeStruct((batch * seq, heads * HEAD_DIM), BF16), kv_shape, kv_shape],
        scratch_shapes=[pltpu.VMEM((hp * tq, HEAD_DIM), F32), pltpu.VMEM((hp * tq, 1), F32),
                        pltpu.VMEM((seq, w), BF16), pltpu.VMEM((seq, w), BF16),
                        pltpu.VMEM((2, hp * tq, tq), BF16), pltpu.VMEM((2, hp * tq, tq), F32),
                        pltpu.VMEM((2, hp * tq, 1), F32)],
        compiler_params=_params(("arbitrary", "arbitrary", "arbitrary")),
        name="sb_prompt",
    )(proj, proj, proj, usuf)


def _sb_decode_kernel(q_ref, k_ref, v_ref, pk_ref, pv_ref, un_ref, up_ref, o_ref, acc_ref, c_ref,
                      *, seq, heads, tk, scale):
    j = pl.program_id(1)
    hs = [slice(h * HEAD_DIM, (h + 1) * HEAD_DIM) for h in range(heads)]
    q = (q_ref[...] * scale).astype(BF16)
    qs = [q[:, hs[h]] for h in range(heads)]

    @pl.when(j == 0)
    def _():
        pad = jnp.zeros((LANES - seq, heads * HEAD_DIM), F32)
        kn = jnp.concatenate([k_ref[...], pad], axis=0).astype(BF16)
        vn = jnp.concatenate([v_ref[...], pad], axis=0).astype(BF16)
        row = lax.broadcasted_iota(jnp.int32, (seq, LANES), 0)
        col = lax.broadcasted_iota(jnp.int32, (seq, LANES), 1)
        mask = _cat([col < row] * heads, 0)
        pv, c = _sb_blocks(qs, [kn[:, hs[h]] for h in range(heads)], [vn[:, hs[h]] for h in range(heads)],
                           un_ref[...], jnp.zeros((heads * seq, 1), F32), mask)
        acc_ref[...] = pv
        c_ref[...] = c

    usuf = up_ref[...]
    for sblk in reversed(range(pk_ref.shape[2] // tk)):
        ks = [pk_ref[0, h, pl.ds(sblk * tk, tk), :].astype(BF16) for h in range(heads)]
        vs = [pv_ref[0, h, pl.ds(sblk * tk, tk), :].astype(BF16) for h in range(heads)]
        pv, c = _sb_blocks(qs, ks, vs, usuf, c_ref[...], None)
        acc_ref[...] += pv
        c_ref[...] = c

    @pl.when(j == pl.num_programs(1) - 1)
    def _():
        o_ref[...] = _cat([acc_ref[pl.ds(h * seq, seq), :] for h in range(heads)], 1).astype(o_ref.dtype)


def _sb_decode(proj, past_k, past_v, batch, seq, heads, col0, tk, tkb):
    past = past_k.shape[2]
    nblk = past // tkb
    w = heads * HEAD_DIM
    un, up = _sb_suffix(LANES), _sb_suffix(tk)
    kv_spec = pl.BlockSpec((1, heads, tkb, HEAD_DIM), lambda b, j: (b, 0, nblk - 1 - j, 0))
    return pl.pallas_call(
        functools.partial(_sb_decode_kernel, seq=seq, heads=heads, tk=tk, scale=HEAD_DIM ** -0.5 * LOG2E),
        grid=(batch, nblk),
        in_specs=[pl.BlockSpec((seq, w), lambda b, j: (b, col0 // heads)),
                  pl.BlockSpec((seq, w), lambda b, j: (b, col0 // heads + 1)),
                  pl.BlockSpec((seq, w), lambda b, j: (b, col0 // heads + 2)),
                  kv_spec, kv_spec,
                  pl.BlockSpec(un.shape, lambda b, j: (0, 0)),
                  pl.BlockSpec(up.shape, lambda b, j: (0, 0))],
        out_specs=pl.BlockSpec((seq, w), lambda b, j: (b, 0)),
        out_shape=jax.ShapeDtypeStruct((batch * seq, w), BF16),
        scratch_shapes=[pltpu.VMEM((heads * seq, HEAD_DIM), F32), pltpu.VMEM((heads * seq, 1), F32)],
        compiler_params=_params(("arbitrary", "arbitrary")),
        name="sb_decode",
    )(proj, proj, proj, past_k, past_v, un, up)


def _route(logits):
    lane = lax.broadcasted_iota(jnp.int32, logits.shape, 1).astype(F32)
    first = lambda hit: jnp.min(jnp.where(hit, lane, float(LANES)), axis=-1, keepdims=True)
    is_g = lane < N_GROUPS
    gl = jnp.where(is_g, logits, NEG_BIG)
    gmax = jnp.max(gl, axis=-1, keepdims=True)
    gidx = first(gl == gmax)
    gsum = jnp.sum(jnp.where(is_g, jnp.exp(gl - gmax), 0.0), axis=-1, keepdims=True)
    gw = 1.0 / gsum
    lo = N_GROUPS + EXPERTS_PER_GROUP * gidx
    in_g = (lane >= lo) & (lane < lo + EXPERTS_PER_GROUP)
    el = jnp.where(in_g, logits, NEG_BIG)
    m1 = jnp.max(el, axis=-1, keepdims=True)
    i1 = first(in_g & (el == m1))
    rest = in_g & (lane != i1)
    el2 = jnp.where(rest, logits, NEG_BIG)
    m2 = jnp.max(el2, axis=-1, keepdims=True)
    i2 = first(rest & (el2 == m2))
    t = jnp.exp(m2 - m1)
    w0 = gw / (1.0 + t)
    w1 = gw * t / (1.0 + t)
    e0 = i1 - N_GROUPS
    e1 = i2 - N_GROUPS
    return jnp.where(lane == 0, e0, jnp.where(lane == 1, e1, jnp.where(lane == 2, w0, jnp.where(lane == 3, w1, 0.0))))


def _outproj_kernel(x_ref, a_ref, b_ref, wa_ref, wb_ref, nw_ref, rw_ref, rb_ref, *rest):
    x2_ref, xn_ref, r_ref = rest[-3:]
    wh = rw_ref[:, :LANES]
    half = min(x_ref.shape[0], 256)
    for h0 in range(0, x_ref.shape[0], half):
        rows = pl.ds(h0, half)
        acc = jnp.dot(a_ref[rows, :], wa_ref[...], preferred_element_type=F32)
        acc = acc + jnp.dot(b_ref[rows, :], wb_ref[...], preferred_element_type=F32)
        x2 = x_ref[rows, :] + acc
        x2_ref[rows, :] = x2
        ms = jnp.mean(x2 * x2, axis=-1, keepdims=True)
        xn = (x2 * lax.rsqrt(ms + NORM_EPS)) * nw_ref[...]
        xn_ref[rows, :] = xn
        xh = xn.astype(BF16)
        xl = (xn - xh.astype(F32)).astype(BF16)
        both = jnp.dot(xh, rw_ref[...], preferred_element_type=F32)
        logits = (both[:, :LANES] + both[:, LANES:] + jnp.dot(xl, wh, preferred_element_type=F32)) + rb_ref[...]
        r_ref[rows, :] = _route(logits)


def _outproj(x, oa, ob, wa, wb, nw, rw, rb, tm, n_total, row0, prev):
    n, d = x.shape
    blk0 = row0 // tm
    row = lambda i: (i, 0)
    orow = lambda i: (blk0 + i, 0)
    const = lambda i: (0, 0)
    in_specs = [pl.BlockSpec((tm, d), row),
                pl.BlockSpec((tm, oa.shape[1]), row),
                pl.BlockSpec((tm, ob.shape[1]), row),
                pl.BlockSpec(wa.shape, const),
                pl.BlockSpec(wb.shape, const),
                pl.BlockSpec((1, d), const),
                pl.BlockSpec(rw.shape, const),
                pl.BlockSpec((1, LANES), const)]
    args = [x, oa, ob, wa, wb, nw, rw, rb]
    aliases = {}
    if prev is not None:
        in_specs += [pl.BlockSpec(memory_space=pl.ANY)] * 3
        aliases = {len(args) + t: t for t in range(3)}
        args += list(prev)
    return pl.pallas_call(
        _outproj_kernel,
        grid=(n // tm,),
        in_specs=in_specs,
        out_specs=[pl.BlockSpec((tm, d), orow), pl.BlockSpec((tm, d), orow), pl.BlockSpec((tm, LANES), orow)],
        out_shape=[jax.ShapeDtypeStruct((n_total, d), F32), jax.ShapeDtypeStruct((n_total, d), F32),
                   jax.ShapeDtypeStruct((n_total, LANES), F32)],
        input_output_aliases=aliases,
        compiler_params=_params(("arbitrary",)),
        name="outproj",
    )(*args)


def _row_out(x_ref, r, dst_hbm, row, sem):
    return pltpu.make_async_copy(x_ref.at[pl.ds(r, 1), :], dst_hbm.at[pl.ds(row, 1), :], sem)


def _zero_fill_copies(z_ref, xs_hbm, start, length, sem):
    sub = 8
    head = jnp.minimum((-start) & (sub - 1), length)
    body0 = start + head
    body = length - head
    tail0 = body0 + (body & -sub)
    row = lambda off: pltpu.make_async_copy(z_ref.at[pl.ds(0, 1), :], xs_hbm.at[pl.ds(off, 1), :], sem)
    pairs = [(k < head, row(start + k)) for k in range(sub - 1)]
    bit = z_ref.shape[0]
    while bit >= sub:
        off = pl.multiple_of(body0 + (body & (-2 * bit)), sub)
        pairs.append(((body & bit) != 0,
                      pltpu.make_async_copy(z_ref.at[pl.ds(0, bit), :], xs_hbm.at[pl.ds(off, bit), :], sem)))
        bit //= 2
    pairs += [(k < (body & (sub - 1)), row(tail0 + k)) for k in range(sub - 1)]
    return pairs


def _scatter_kernel(ps_ref, pl_ref, pos_ref, x_ref, xs_hbm, z_ref, sem, zsem, *, tm):
    @pl.when(pl.program_id(0) == 0)
    def _():
        z_ref[...] = jnp.zeros_like(z_ref)

        def fill(e, carry):
            for pred, cp in _zero_fill_copies(z_ref, xs_hbm, ps_ref[e], pl_ref[e], zsem):
                pl.when(pred)(cp.start)
            return carry
        lax.fori_loop(0, N_EXPERTS, fill, 0)

    @pl.when(pl.program_id(0) == pl.num_programs(0) - 1)
    def _():
        def settle(e, carry):
            for pred, cp in _zero_fill_copies(z_ref, xs_hbm, ps_ref[e], pl_ref[e], zsem):
                pl.when(pred)(cp.wait)
            return carry
        lax.fori_loop(0, N_EXPERTS, settle, 0)

    def issue(r, carry):
        _row_out(x_ref, r, xs_hbm, pos_ref[0, 0, 2 * r], sem).start()
        _row_out(x_ref, r, xs_hbm, pos_ref[0, 0, 2 * r + 1], sem).start()
        return carry
    lax.fori_loop(0, tm, issue, 0, unroll=8)

    def drain(r, carry):
        _row_out(x_ref, r, xs_hbm, 0, sem).wait()
        _row_out(x_ref, r, xs_hbm, 0, sem).wait()
        return carry
    lax.fori_loop(0, tm, drain, 0, unroll=8)


def _scatter(pad_start, pad_len, pos, x, tm, rows, tm_rows):
    n, d = x.shape
    return pl.pallas_call(
        functools.partial(_scatter_kernel, tm=tm),
        grid_spec=pltpu.PrefetchScalarGridSpec(
            num_scalar_prefetch=2,
            grid=(n // tm,),
            in_specs=[pl.BlockSpec((1, 1, 2 * tm), lambda m, *_: (m, 0, 0), memory_space=pltpu.SMEM),
                      pl.BlockSpec((tm, d), lambda m, *_: (m, 0))],
            out_specs=pl.BlockSpec(memory_space=pl.ANY),
            scratch_shapes=[pltpu.VMEM((tm_rows // 2, d), x.dtype), pltpu.SemaphoreType.DMA(()),
                            pltpu.SemaphoreType.DMA(())]),
        out_shape=jax.ShapeDtypeStruct((rows, d), x.dtype),
        compiler_params=_params(("arbitrary",)),
        name="scatter",
    )(pad_start, pad_len, pos, x)


def _weight_copies(e, slot, w_hbm, w_stage, sem):
    return [pltpu.make_async_copy(w_hbm[t].at[e], w_stage[t].at[slot], sem.at[slot, t]) for t in range(3)]


def _experts_kernel(te_ref, nu_ref, nx_ref, ig_ref, x_ref, wg_hbm, wu_hbm, wd_hbm, o_ref,
                    sg, su, sd, g_scr, u_scr, d_scr, sem):
    m = pl.program_id(0)
    w_hbm, w_stage = (wg_hbm, wu_hbm, wd_hbm), (sg, su, sd)

    @pl.when(m < nu_ref[0])
    def _():
        e = te_ref[m]
        new_expert = (m == 0) | (e != te_ref[jnp.maximum(m - 1, 0)])
        slot = ig_ref[m] % 2

        @pl.when(m == 0)
        def _():
            for cp in _weight_copies(e, slot, w_hbm, w_stage, sem):
                cp.start()

        @pl.when(new_expert)
        def _():
            for cp in _weight_copies(e, slot, w_hbm, w_stage, sem):
                cp.wait()

            @pl.when(nx_ref[m] != e)
            def _():
                for cp in _weight_copies(nx_ref[m], 1 - slot, w_hbm, w_stage, sem):
                    cp.start()

            g_scr[...] = sg[slot].astype(BF16)
            u_scr[...] = su[slot].astype(BF16)
            d_scr[...] = sd[slot].astype(BF16)

        x = x_ref[...].astype(BF16)
        g = jnp.dot(x, g_scr[...], preferred_element_type=F32)
        u = jnp.dot(x, u_scr[...], preferred_element_type=F32)
        hid = ((g * _sigmoid(g)) * u).astype(BF16)
        o_ref[...] = jnp.dot(hid, d_scr[...], preferred_element_type=F32)


def _experts(tile_expert, n_used, tile_next, tile_group, xs, wg, wu, wd, tm):
    rows, d = xs.shape
    ff = wg.shape[2]
    tile = lambda m, te, nu, *_: (jnp.minimum(m, jnp.maximum(nu[0] - 1, 0)), 0)
    hbm = pl.BlockSpec(memory_space=pl.ANY)
    return pl.pallas_call(
        _experts_kernel,
        grid_spec=pltpu.PrefetchScalarGridSpec(
            num_scalar_prefetch=4,
            grid=(rows // tm,),
            in_specs=[pl.BlockSpec((tm, d), tile), hbm, hbm, hbm],
            out_specs=pl.BlockSpec((tm, d), tile),
            scratch_shapes=[pltpu.VMEM((2, d, ff), F32), pltpu.VMEM((2, d, ff), F32), pltpu.VMEM((2, ff, d), F32),
                            pltpu.VMEM((d, ff), BF16), pltpu.VMEM((d, ff), BF16), pltpu.VMEM((ff, d), BF16),
                            pltpu.SemaphoreType.DMA((2, 3))]),
        out_shape=jax.ShapeDtypeStruct((rows, d), F32),
        compiler_params=_params(("arbitrary",)),
        name="experts",
    )(tile_expert, n_used, tile_next, tile_group, xs, wg, wu, wd)


def _row_in(src_hbm, row, dst, r, sem):
    return pltpu.make_async_copy(src_hbm.at[pl.ds(row, 1), :], dst.at[pl.ds(r, 1), :], sem)


def _combine_kernel(pos_ref, nxt_ref, x_ref, r_ref, fw_ref, ys_hbm, o_ref, buf, sem, *, tm, ntiles):
    m = pl.program_id(0)
    slot = m % 2

    def issue(tbl, s):
        def go(r, carry):
            _row_in(ys_hbm, tbl[0, 0, 2 * r], buf.at[s, 0], r, sem.at[s]).start()
            _row_in(ys_hbm, tbl[0, 0, 2 * r + 1], buf.at[s, 1], r, sem.at[s]).start()
            return carry
        lax.fori_loop(0, tm, go, 0, unroll=8)

    @pl.when(m == 0)
    def _():
        issue(pos_ref, 0)

    @pl.when(m + 1 < ntiles)
    def _():
        issue(nxt_ref, 1 - slot)

    def drain(r, carry):
        _row_in(ys_hbm, 0, buf.at[slot, 0], r, sem.at[slot]).wait()
        _row_in(ys_hbm, 0, buf.at[slot, 1], r, sem.at[slot]).wait()
        return carry
    lax.fori_loop(0, tm, drain, 0, unroll=8)
    rt = r_ref[...]
    x = x_ref[...] + (buf[slot, 0] * rt[:, 2:3] + buf[slot, 1] * rt[:, 3:4])
    ms = jnp.mean(x * x, axis=-1, keepdims=True)
    o_ref[...] = (x * lax.rsqrt(ms + NORM_EPS)) * fw_ref[...]


def _combine(pos, x2, r, fw, ys, tm, row0, n):
    d = x2.shape[1]
    blk0 = row0 // tm
    ntiles = n // tm
    smem = functools.partial(pl.BlockSpec, memory_space=pltpu.SMEM)
    return pl.pallas_call(
        functools.partial(_combine_kernel, tm=tm, ntiles=ntiles),
        grid=(ntiles,),
        in_specs=[smem((1, 1, 2 * tm), lambda m: (blk0 + m, 0, 0)),
                  smem((1, 1, 2 * tm), lambda m: (blk0 + jnp.minimum(m + 1, ntiles - 1), 0, 0)),
                  pl.BlockSpec((tm, d), lambda m: (blk0 + m, 0)),
                  pl.BlockSpec((tm, LANES), lambda m: (blk0 + m, 0)),
                  pl.BlockSpec((1, d), lambda m: (0, 0)),
                  pl.BlockSpec(memory_space=pl.ANY)],
        out_specs=pl.BlockSpec((tm, d), lambda m: (m, 0)),
        out_shape=jax.ShapeDtypeStruct((n, d), F32),
        scratch_shapes=[pltpu.VMEM((2, 2, tm, d), F32), pltpu.SemaphoreType.DMA((2,))],
        compiler_params=_params(("arbitrary",)),
        name="combine",
    )(pos, pos, x2, r, fw, ys)


def _dispatch_tables(r, tm):
    n = r.shape[0]
    tiles = (2 * n) // tm + N_EXPERTS
    ids = jnp.arange(N_EXPERTS, dtype=jnp.int32)
    e = r[:, 0:2].astype(jnp.int32).reshape(-1)
    onehot = (e[:, None] == ids[None, :]).astype(jnp.int32)
    incl = jnp.cumsum(onehot, axis=0)
    counts = incl[-1]
    etiles = (counts + tm - 1) // tm
    tile_end = jnp.cumsum(etiles)
    tile_start = tile_end - etiles
    n_used = tile_end[-1]
    pos = jnp.sum(onehot * (incl - 1 + (tile_start * tm)[None, :]), axis=1)
    m = jnp.minimum(jnp.arange(tiles, dtype=jnp.int32), n_used - 1)
    te = jnp.sum((tile_end[None, :] <= m[:, None]).astype(jnp.int32), axis=1)
    sel = (te[:, None] == ids[None, :]).astype(jnp.int32)
    pick = lambda tbl: jnp.sum(sel * tbl[None, :], axis=1)
    used = counts > 0
    later = used[None, :] & (ids[None, :] > ids[:, None])
    next_used = jnp.min(jnp.where(later, ids[None, :], N_EXPERTS), axis=1)
    next_used = jnp.where(next_used == N_EXPERTS, ids, next_used)
    group = jnp.cumsum(used.astype(jnp.int32)) - 1
    i32 = lambda a: a.astype(jnp.int32)
    return (i32(pos), i32(te), i32(n_used).reshape(1), i32(pick(next_used)), i32(pick(group)),
            i32(tile_start * tm + counts), i32(etiles * tm - counts))


_TILES = dict(
    token_tile=256,
    expert_row_tile=256,
    inproj_column_blocks=2,
    hgrn_heads=4,
    hgrn_time_block=1024,
    sb_block=256,
    sb_heads=8,
    decode_key_block=512,
    decode_dma_block=2048,
)

def kernel(x_prompt, x_sample, cache_sb_k, cache_sb_v, state_hgrn, w_in, hg_lb_logits, hg_norm_w, w_out, norm1_w,
           norm2_w, router_group_w, router_group_b, router_expert_w, router_expert_b, expert_w_gate, expert_w_up,
           expert_w_down, final_norm_w):
    bp, tp, d = x_prompt.shape
    bs, ts, _ = x_sample.shape
    np_, ns = bp * tp, bs * ts
    n = np_ + ns
    heads = w_in.shape[2] // (7 * HEAD_DIM)
    hw = heads * HEAD_DIM
    cfg = _TILES
    tm_tok, tm_moe = cfg["token_tile"], cfg["expert_row_tile"]

    w_in_b = w_in[0].astype(BF16)
    w_out_b = w_out[0].astype(BF16)
    n1 = norm1_w[0].reshape(1, d)
    n2 = norm2_w[0].reshape(1, d)
    hnw = hg_norm_w[0].reshape(1, HEAD_DIM)
    fw = final_norm_w.reshape(1, d)

    xp = x_prompt.reshape(np_, d)
    xs = x_sample.reshape(ns, d)
    w_cols = w_in_b.shape[1] // cfg["inproj_column_blocks"]
    proj_p = _inproj(xp, n1, w_in_b, tm_tok, w_cols)
    proj_s = _inproj(xs, n1, w_in_b, min(ns, tm_tok), w_cols)

    ohg_p, st_p = _hgrn(proj_p, hg_lb_logits, hnw, None, bp, tp, heads, 0, cfg["hgrn_heads"],
                        min(tp, cfg["hgrn_time_block"]))
    ohg_s, st_s = _hgrn(proj_s, hg_lb_logits, hnw, state_hgrn[0], bs, ts, heads, 0, heads, ts)
    osb_p, k_p, v_p = _sb_prompt(proj_p, bp, tp, heads, 4 * heads, cfg["sb_block"], cfg["sb_heads"])
    osb_s = _sb_decode(proj_s, cache_sb_k[0], cache_sb_v[0], bs, ts, heads, 4 * heads,
                       cfg["decode_key_block"], cfg["decode_dma_block"])

    rw = jnp.concatenate([router_group_w[0], router_expert_w[0]], axis=1)
    rw = jnp.pad(rw, ((0, 0), (0, LANES - rw.shape[1])))
    rwh = rw.astype(BF16)
    rw2 = jnp.concatenate([rwh, (rw - rwh.astype(F32)).astype(BF16)], axis=1)
    rb = jnp.concatenate([router_group_b[0], router_expert_b[0]])
    rb = jnp.pad(rb, (0, LANES - rb.shape[0])).reshape(1, LANES)

    wa, wb = w_out_b[:hw], w_out_b[hw:]
    outs = _outproj(xp, ohg_p, osb_p, wa, wb, n2, rw2, rb, 2 * tm_tok, n, 0, None)
    x2, xn2, r = _outproj(xs, ohg_s, osb_s, wa, wb, n2, rw2, rb, tm_tok, n, np_, outs)

    pos, tile_expert, n_used, tile_next, tile_group, pad_start, pad_len = _dispatch_tables(r, tm_moe)
    pos = pos.reshape(n // tm_tok, 1, 2 * tm_tok)
    xg = _scatter(pad_start, pad_len, pos, xn2, tm_tok, tile_expert.shape[0] * tm_moe, tm_moe)
    ys = _experts(tile_expert, n_used, tile_next, tile_group, xg,
                  expert_w_gate[0], expert_w_up[0], expert_w_down[0], tm_moe)
    y_p = _combine(pos, x2, r, fw, ys, tm_tok, 0, np_)
    y_s = _combine(pos, x2, r, fw, ys, tm_tok, np_, ns)

    heads_s = lambda a: a.reshape(bs, ts, heads, HEAD_DIM).transpose(0, 2, 1, 3)[None]
    k_s = heads_s(proj_s[:, 5 * hw:6 * hw])
    v_s = heads_s(proj_s[:, 6 * hw:7 * hw])
    return (y_p.reshape(bp, tp, d), y_s.reshape(bs, ts, d), k_p[None], v_p[None], st_p[None],
            k_s, v_s, st_s[None])
```

```python
import functools

import jax
import jax.numpy as jnp
import numpy as np
from jax import lax
from jax.experimental import pallas as pl
from jax.experimental.pallas import tpu as pltpu

F32 = jnp.float32
BF16 = jnp.bfloat16
NORM_EPS = 1e-6
LANES = 128
HEAD_DIM = 128
HGRN_CHUNK = 64
HGRN_SUB = 16
N_GROUPS = 4
EXPERTS_PER_GROUP = 8
N_EXPERTS = N_GROUPS * EXPERTS_PER_GROUP
VMEM_LIMIT = 48 * 1024 * 1024
NEG_BIG = -1e30
LOG2E = 1.4426950408889634

_NT = (((1,), (1,)), ((), ()))
_TN = (((0,), (0,)), ((), ()))


def _params(sem):
    return pltpu.CompilerParams(dimension_semantics=sem, vmem_limit_bytes=VMEM_LIMIT)


def _sigmoid(x):
    return 1.0 / (1.0 + jnp.exp(-x))


def _split3(x):
    h = x.astype(BF16)
    r = x - h.astype(F32)
    m = r.astype(BF16)
    l = (r - m.astype(F32)).astype(BF16)
    return h, m, l


def _cat(parts, axis):
    return parts[0] if len(parts) == 1 else jnp.concatenate(parts, axis=axis)


def _inproj_kernel(x_ref, nw_ref, w_ref, o_ref):
    x = x_ref[...]
    ms = jnp.mean(x * x, axis=-1, keepdims=True)
    xn = ((x * lax.rsqrt(ms + NORM_EPS)) * nw_ref[...]).astype(BF16)
    o_ref[...] = jnp.dot(xn, w_ref[...], preferred_element_type=F32)


def _inproj(x, nw, w, tm, tn):
    n, d = x.shape
    c = w.shape[1]
    return pl.pallas_call(
        _inproj_kernel,
        grid=(c // tn, n // tm),
        in_specs=[pl.BlockSpec((tm, d), lambda j, i: (i, 0)),
                  pl.BlockSpec((1, d), lambda j, i: (0, 0)),
                  pl.BlockSpec((d, tn), lambda j, i: (0, j))],
        out_specs=pl.BlockSpec((tm, tn), lambda j, i: (i, j)),
        out_shape=jax.ShapeDtypeStruct((n, c), F32),
        compiler_params=_params(("arbitrary", "arbitrary")),
        name="inproj",
    )(x, nw, w)


def _hgrn_kernel(*refs, tb, chunk, sub, hp, has_state):
    if has_state:
        (hq_ref, hf_ref, hi_ref, hg_ref, lbl_ref, nw_ref, tri_ref, wsel_ref, s0_ref,
         o_ref, sout_ref, st_ref, q_scr, b_scr, c_scr) = refs
    else:
        (hq_ref, hf_ref, hi_ref, hg_ref, lbl_ref, nw_ref, tri_ref, wsel_ref,
         o_ref, sout_ref, st_ref, q_scr, b_scr, c_scr) = refs
    n_sub = chunk // sub
    ti = pl.program_id(2)
    hs = [slice(h * HEAD_DIM, (h + 1) * HEAD_DIM) for h in range(hp)]

    lg = lbl_ref[...]
    lg = jnp.exp(lg - jnp.max(lg, axis=0, keepdims=True))
    lb = lg[0:1, :] / jnp.sum(lg, axis=0, keepdims=True)

    @pl.when(ti == 0)
    def _():
        for h in range(hp):
            st_ref[h] = s0_ref[0, h].T if has_state else jnp.zeros((HEAD_DIM, HEAD_DIM), F32)

    row = lax.broadcasted_iota(jnp.int32, (chunk, chunk), 0)
    col = lax.broadcasted_iota(jnp.int32, (chunk, chunk), 1)
    blk_diff = row // sub - col // sub
    diag_mask = (blk_diff == 0) & (col <= row)
    diag_mask_all = _cat([diag_mask] * hp, 0)
    tri = tri_ref[...]
    wsel = wsel_ref[...]
    nw = nw_ref[...]

    def prepare(ci, slot):
        r0 = pl.multiple_of(ci * chunk, chunk)
        hq = hq_ref[pl.ds(r0, chunk), :]
        hf = hf_ref[pl.ds(r0, chunk), :]
        f = lb + (1.0 - lb) * _sigmoid(hf)
        k = 1.0 - f
        l1, l2, l3 = _split3(jnp.log(f) * LOG2E)
        b = (jnp.dot(tri, l1, preferred_element_type=F32) + jnp.dot(tri, l2, preferred_element_type=F32)
             + jnp.dot(tri, l3, preferred_element_type=F32))
        q_scr[slot] = hq * _sigmoid(hq)
        b_scr[slot] = b
        c_scr[slot] = b - jnp.log(k) * LOG2E

    def main(ci, slot):
        r0 = pl.multiple_of(ci * chunk, chunk)
        v = hi_ref[pl.ds(r0, chunk), :]
        hg = hg_ref[pl.ds(r0, chunk), :]
        q = q_scr[slot]
        b = b_scr[slot]
        c = c_scr[slot]
        b_rows = b_scr.at[slot]
        c_rows = c_scr.at[slot]
        vb = v.astype(BF16)
        b_last = b_rows[pl.ds(chunk - 1, 1), :]
        qd = (q * jnp.exp2(b)).astype(BF16)
        kd = jnp.exp2(b_last - c).astype(BF16)
        s_decay = jnp.exp2(b_last)

        slabs = []
        for r in range(sub):
            parts = []
            for i in range(n_sub):
                c_row = c_rows[pl.ds(i * sub + r, 1), :]
                sl = slice(i * sub, (i + 1) * sub)
                parts.append(q[sl] * jnp.exp2(jnp.minimum(b[sl] - c_row, 0.0)))
            slabs.append(_cat(parts, 0).astype(BF16))
        xcat = _cat([_cat([s[:, hs[h]] for s in slabs], 1) for h in range(hp)], 0)
        a_all = jnp.where(diag_mask_all, jnp.dot(xcat, wsel, preferred_element_type=F32), 0.0)

        if n_sub > 1:
            starts = [None] + [b_rows[pl.ds(i * sub - 1, 1), :] for i in range(1, n_sub)]
            ends = [b_rows[pl.ds(i * sub + sub - 1, 1), :] for i in range(n_sub)]
            qh, kh = [], []
            for i in range(n_sub):
                sl = slice(i * sub, (i + 1) * sub)
                qh.append(q[sl] if i == 0 else q[sl] * jnp.exp2(b[sl] - starts[i]))
                kh.append(jnp.exp2(ends[i] - c[sl]))
            khat = _cat(kh, 0).astype(BF16)
            qds = []
            for d in range(1, n_sub):
                parts = []
                for i in range(n_sub):
                    if i < d:
                        parts.append(jnp.zeros((sub, hp * HEAD_DIM), F32))
                    elif d == 1:
                        parts.append(qh[i])
                    else:
                        parts.append(qh[i] * jnp.exp2(starts[i] - ends[i - d]))
                qds.append(_cat(parts, 0).astype(BF16))
            qstack = _cat(qds, 0)

        outs = []
        for h in range(hp):
            st = st_ref[h]
            o = lax.dot_general(qd[:, hs[h]], st.astype(BF16), _NT, preferred_element_type=F32)
            st_ref[h] = st * s_decay[:, hs[h]] + lax.dot_general(vb[:, hs[h]], kd[:, hs[h]], _TN,
                                                                 preferred_element_type=F32)
            a = a_all[h * chunk:(h + 1) * chunk]
            if n_sub > 1:
                m = lax.dot_general(qstack[:, hs[h]], khat[:, hs[h]], _NT, preferred_element_type=F32)
                for d in range(1, n_sub):
                    a = a + jnp.where(blk_diff == d, m[(d - 1) * chunk:d * chunk], 0.0)
            o = o + jnp.dot(a.astype(BF16), vb[:, hs[h]], preferred_element_type=F32)
            ms = jnp.mean(o * o, axis=-1, keepdims=True)
            outs.append((o * lax.rsqrt(ms + NORM_EPS)) * nw[:, hs[h]])
        g = hg * _sigmoid(hg)
        o_ref[pl.ds(r0, chunk), :] = (_cat(outs, 1) * g).astype(o_ref.dtype)

    n_chunks = tb // chunk
    prepare(0, 0)
    if n_chunks == 1:
        main(0, 0)
    else:
        def pair(pp, carry):
            ci = 2 * pp
            prepare(ci + 1, 1)
            main(ci, 0)
            prepare(jnp.minimum(ci + 2, n_chunks - 1), 0)
            main(ci + 1, 1)
            return carry

        lax.fori_loop(0, n_chunks // 2, pair, 0)

    @pl.when(ti == pl.num_programs(2) - 1)
    def _():
        for h in range(hp):
            sout_ref[0, h] = st_ref[h].T


def _hgrn_consts(chunk, sub):
    t = np.arange(chunk)
    tri = (t[None, :] <= t[:, None]).astype(np.float32)
    rows = np.arange(sub * HEAD_DIM) // HEAD_DIM
    wsel = (rows[:, None] == (t[None, :] % sub)).astype(np.float32)
    return jnp.asarray(tri, BF16), jnp.asarray(wsel, BF16)


def _hgrn(proj, lb_logits, norm_w, state, batch, seq, heads, col0, hp, tb):
    chunk = min(HGRN_CHUNK, seq)
    sub = min(HGRN_SUB, chunk)
    tri, wsel = _hgrn_consts(chunk, sub)
    has_state = state is not None
    nt = seq // tb
    w = hp * HEAD_DIM

    def col_spec(g):
        return pl.BlockSpec((tb, w), lambda b, h, t: (b * nt + t, (col0 + g * heads) // hp + h))

    const = lambda b, h, t: (0, 0)
    in_specs = [col_spec(0), col_spec(1), col_spec(2), col_spec(3),
                pl.BlockSpec((lb_logits.shape[0], w), lambda b, h, t: (0, h)),
                pl.BlockSpec((1, w), const),
                pl.BlockSpec(tri.shape, const),
                pl.BlockSpec(wsel.shape, const)]
    args = [proj, proj, proj, proj, lb_logits, jnp.tile(norm_w, (1, hp)), tri, wsel]
    if has_state:
        in_specs.append(pl.BlockSpec((1, hp, HEAD_DIM, HEAD_DIM), lambda b, h, t: (b, h, 0, 0)))
        args.append(state)
    return pl.pallas_call(
        functools.partial(_hgrn_kernel, tb=tb, chunk=chunk, sub=sub, hp=hp, has_state=has_state),
        grid=(batch, heads // hp, nt),
        in_specs=in_specs,
        out_specs=[pl.BlockSpec((tb, w), lambda b, h, t: (b * nt + t, h)),
                   pl.BlockSpec((1, hp, HEAD_DIM, HEAD_DIM), lambda b, h, t: (b, h, 0, 0))],
        out_shape=[jax.ShapeDtypeStruct((batch * seq, heads * HEAD_DIM), BF16),
                   jax.ShapeDtypeStruct((batch, heads, HEAD_DIM, HEAD_DIM), F32)],
        scratch_shapes=[pltpu.VMEM((hp, HEAD_DIM, HEAD_DIM), F32),
                        pltpu.VMEM((2, chunk, w), F32),
                        pltpu.VMEM((2, chunk, w), F32),
                        pltpu.VMEM((2, chunk, w), F32)],
        compiler_params=_params(("arbitrary", "arbitrary", "arbitrary")),
        name="hgrn_state" if has_state else "hgrn",
    )(*args)


def _sb_blocks(qs, ks, vs, u, c, mask):
    tq = qs[0].shape[0]
    z = _cat([lax.dot_general(q, k, _NT, preferred_element_type=F32) for q, k in zip(qs, ks)], 0)
    nz = -z
    sp = jnp.maximum(nz, 0.0) + jnp.log2(1.0 + jnp.exp2(jnp.minimum(z, nz)))
    l = nz - sp
    if mask is not None:
        l = jnp.where(mask, l, 0.0)
    s = jnp.dot(l.astype(BF16), u, preferred_element_type=F32)
    p = jnp.exp2(s + c - sp)
    if mask is not None:
        p = jnp.where(mask, p, 0.0)
    p = p.astype(BF16)
    pv = _cat([jnp.dot(p[h * tq:(h + 1) * tq], vs[h], preferred_element_type=F32) for h in range(len(qs))], 0)
    return pv, c + (s[:, 0:1] + l[:, 0:1])


def _sb_suffix(tk):
    s = np.arange(tk)
    u = (s[:, None] > s[None, :]).astype(np.float32)
    return jnp.asarray(u, BF16)


def _sb_prompt_kernel(q_ref, k_ref, v_ref, u_ref, o_ref, ko_ref, vo_ref, acc_ref, c_ref, kb_ref, vb_ref,
                      lc_ref, sp_ref, l0_ref, *, tq, hp, scale):
    i = pl.program_id(2)
    hs = [slice(h * HEAD_DIM, (h + 1) * HEAD_DIM) for h in range(hp)]

    for h in range(hp):
        ko_ref[0, h] = k_ref[:, hs[h]]
        vo_ref[0, h] = v_ref[:, hs[h]]
    own = pl.ds(pl.multiple_of(i * tq, tq), tq)
    kb_ref[own, :] = k_ref[...].astype(BF16)
    vb_ref[own, :] = v_ref[...].astype(BF16)

    q = (q_ref[...] * scale).astype(BF16)
    usuf = u_ref[...]

    def scores(j, slot, diagonal):
        kb = kb_ref[pl.ds(pl.multiple_of(j * tq, tq), tq), :]
        z = _cat([lax.dot_general(q[:, hs[h]], kb[:, hs[h]], _NT, preferred_element_type=F32) for h in range(hp)], 0)
        nz = -z
        sp = jnp.maximum(nz, 0.0) + jnp.log2(1.0 + jnp.exp2(jnp.minimum(z, nz)))
        l = nz - sp
        if diagonal:
            row = lax.broadcasted_iota(jnp.int32, (tq, tq), 0)
            col = lax.broadcasted_iota(jnp.int32, (tq, tq), 1)
            mask = _cat([col < row] * hp, 0)
            l = jnp.where(mask, l, 0.0)
            sp = jnp.where(mask, sp, -NEG_BIG)
        lc_ref[slot] = l.astype(BF16)
        sp_ref[slot] = sp
        l0_ref[slot] = l[:, 0:1]

    def values(j, slot):
        vb = vb_ref[pl.ds(pl.multiple_of(j * tq, tq), tq), :]
        s = jnp.dot(lc_ref[slot], usuf, preferred_element_type=F32)
        c = c_ref[...]
        p = jnp.exp2(s + c - sp_ref[slot]).astype(BF16)
        acc_ref[...] += _cat([jnp.dot(p[h * tq:(h + 1) * tq], vb[:, hs[h]], preferred_element_type=F32)
                              for h in range(hp)], 0)
        c_ref[...] = c + (s[:, 0:1] + l0_ref[slot])

    acc_ref[...] = jnp.zeros_like(acc_ref)
    c_ref[...] = jnp.zeros_like(c_ref)
    scores(i, 0, True)

    def pair(pp, carry):
        j = i - 1 - 2 * pp
        scores(j, 1, False)
        values(j + 1, 0)
        scores(j - 1, 0, False)
        values(j, 1)
        return carry

    lax.fori_loop(0, i // 2, pair, 0)

    @pl.when(i % 2 == 1)
    def _():
        scores(0, 1, False)
        values(1, 0)
        values(0, 1)

    @pl.when(i % 2 == 0)
    def _():
        values(0, 0)
    o_ref[...] = _cat([acc_ref[pl.ds(h * tq, tq), :] for h in range(hp)], 1).astype(o_ref.dtype)


def _sb_prompt(proj, batch, seq, heads, col0, tq, hp):
    nq = seq // tq
    w = hp * HEAD_DIM
    usuf = _sb_suffix(tq)
    kv_shape = jax.ShapeDtypeStruct((batch, heads, seq, HEAD_DIM), F32)
    return pl.pallas_call(
        functools.partial(_sb_prompt_kernel, tq=tq, hp=hp, scale=HEAD_DIM ** -0.5 * LOG2E),
        grid=(batch, heads // hp, nq),
        in_specs=[pl.BlockSpec((tq, w), lambda b, h, i: (b * nq + i, col0 // hp + h)),
                  pl.BlockSpec((tq, w), lambda b, h, i: (b * nq + i, (col0 + heads) // hp + h)),
                  pl.BlockSpec((tq, w), lambda b, h, i: (b * nq + i, (col0 + 2 * heads) // hp + h)),
                  pl.BlockSpec(usuf.shape, lambda b, h, i: (0, 0))],
        out_specs=[pl.BlockSpec((tq, w), lambda b, h, i: (b * nq + i, h)),
                   pl.BlockSpec((1, hp, tq, HEAD_DIM), lambda b, h, i: (b, h, i, 0)),
                   pl.BlockSpec((1, hp, tq, HEAD_DIM), lambda b, h, i: (b, h, i, 0))],
        out_shape=[jax.ShapeDtypeStruct((batch * seq, heads * HEAD_DIM), BF16), kv_shape, kv_shape],
        scratch_shapes=[pltpu.VMEM((hp * tq, HEAD_DIM), F32), pltpu.VMEM((hp * tq, 1), F32),
                        pltpu.VMEM((seq, w), BF16), pltpu.VMEM((seq, w), BF16),
                        pltpu.VMEM((2, hp * tq, tq), BF16), pltpu.VMEM((2, hp * tq, tq), F32),
                        pltpu.VMEM((2, hp * tq, 1), F32)],
        compiler_params=_params(("arbitrary", "arbitrary", "arbitrary")),
        name="sb_prompt",
    )(proj, proj, proj, usuf)


def _sb_decode_kernel(q_ref, k_ref, v_ref, pk_ref, pv_ref, un_ref, up_ref, o_ref, acc_ref, c_ref,
                      *, seq, heads, tk, scale):
    j = pl.program_id(1)
    hs = [slice(h * HEAD_DIM, (h + 1) * HEAD_DIM) for h in range(heads)]
    q = (q_ref[...] * scale).astype(BF16)
    qs = [q[:, hs[h]] for h in range(heads)]

    @pl.when(j == 0)
    def _():
        pad = jnp.zeros((LANES - seq, heads * HEAD_DIM), F32)
        kn = jnp.concatenate([k_ref[...], pad], axis=0).astype(BF16)
        vn = jnp.concatenate([v_ref[...], pad], axis=0).astype(BF16)
        row = lax.broadcasted_iota(jnp.int32, (seq, LANES), 0)
        col = lax.broadcasted_iota(jnp.int32, (seq, LANES), 1)
        mask = _cat([col < row] * heads, 0)
        pv, c = _sb_blocks(qs, [kn[:, hs[h]] for h in range(heads)], [vn[:, hs[h]] for h in range(heads)],
                           un_ref[...], jnp.zeros((heads * seq, 1), F32), mask)
        acc_ref[...] = pv
        c_ref[...] = c

    usuf = up_ref[...]
    for sblk in reversed(range(pk_ref.shape[2] // tk)):
        ks = [pk_ref[0, h, pl.ds(sblk * tk, tk), :].astype(BF16) for h in range(heads)]
        vs = [pv_ref[0, h, pl.ds(sblk * tk, tk), :].astype(BF16) for h in range(heads)]
        pv, c = _sb_blocks(qs, ks, vs, usuf, c_ref[...], None)
        acc_ref[...] += pv
        c_ref[...] = c

    @pl.when(j == pl.num_programs(1) - 1)
    def _():
        o_ref[...] = _cat([acc_ref[pl.ds(h * seq, seq), :] for h in range(heads)], 1).astype(o_ref.dtype)


def _sb_decode(proj, past_k, past_v, batch, seq, heads, col0, tk, tkb):
    past = past_k.shape[2]
    nblk = past // tkb
    w = heads * HEAD_DIM
    un, up = _sb_suffix(LANES), _sb_suffix(tk)
    kv_spec = pl.BlockSpec((1, heads, tkb, HEAD_DIM), lambda b, j: (b, 0, nblk - 1 - j, 0))
    return pl.pallas_call(
        functools.partial(_sb_decode_kernel, seq=seq, heads=heads, tk=tk, scale=HEAD_DIM ** -0.5 * LOG2E),
        grid=(batch, nblk),
        in_specs=[pl.BlockSpec((seq, w), lambda b, j: (b, col0 // heads)),
                  pl.BlockSpec((seq, w), lambda b, j: (b, col0 // heads + 1)),
                  pl.BlockSpec((seq, w), lambda b, j: (b, col0 // heads + 2)),
                  kv_spec, kv_spec,
                  pl.BlockSpec(un.shape, lambda b, j: (0, 0)),
                  pl.BlockSpec(up.shape, lambda b, j: (0, 0))],
        out_specs=pl.BlockSpec((seq, w), lambda b, j: (b, 0)),
        out_shape=jax.ShapeDtypeStruct((batch * seq, w), BF16),
        scratch_shapes=[pltpu.VMEM((heads * seq, HEAD_DIM), F32), pltpu.VMEM((heads * seq, 1), F32)],
        compiler_params=_params(("arbitrary", "arbitrary")),
        name="sb_decode",
    )(proj, proj, proj, past_k, past_v, un, up)


def _route(logits):
    lane = lax.broadcasted_iota(jnp.int32, logits.shape, 1).astype(F32)
    first = lambda hit: jnp.min(jnp.where(hit, lane, float(LANES)), axis=-1, keepdims=True)
    is_g = lane < N_GROUPS
    gl = jnp.where(is_g, logits, NEG_BIG)
    gmax = jnp.max(gl, axis=-1, keepdims=True)
    gidx = first(gl == gmax)
    gsum = jnp.sum(jnp.where(is_g, jnp.exp(gl - gmax), 0.0), axis=-1, keepdims=True)
    gw = 1.0 / gsum
    lo = N_GROUPS + EXPERTS_PER_GROUP * gidx
    in_g = (lane >= lo) & (lane < lo + EXPERTS_PER_GROUP)
    el = jnp.where(in_g, logits, NEG_BIG)
    m1 = jnp.max(el, axis=-1, keepdims=True)
    i1 = first(in_g & (el == m1))
    rest = in_g & (lane != i1)
    el2 = jnp.where(rest, logits, NEG_BIG)
    m2 = jnp.max(el2, axis=-1, keepdims=True)
    i2 = first(rest & (el2 == m2))
    t = jnp.exp(m2 - m1)
    w0 = gw / (1.0 + t)
    w1 = gw * t / (1.0 + t)
    e0 = i1 - N_GROUPS
    e1 = i2 - N_GROUPS
    return jnp.where(lane == 0, e0, jnp.where(lane == 1, e1, jnp.where(lane == 2, w0, jnp.where(lane == 3, w1, 0.0))))


def _outproj_kernel(x_ref, a_ref, b_ref, wa_ref, wb_ref, nw_ref, rw_ref, rb_ref, *rest):
    x2_ref, xn_ref, r_ref = rest[-3:]
    wh = rw_ref[:, :LANES]
    half = min(x_ref.shape[0], 256)
    for h0 in range(0, x_ref.shape[0], half):
        rows = pl.ds(h0, half)
        acc = jnp.dot(a_ref[rows, :], wa_ref[...], preferred_element_type=F32)
        acc = acc + jnp.dot(b_ref[rows, :], wb_ref[...], preferred_element_type=F32)
        x2 = x_ref[rows, :] + acc
        x2_ref[rows, :] = x2
        ms = jnp.mean(x2 * x2, axis=-1, keepdims=True)
        xn = (x2 * lax.rsqrt(ms + NORM_EPS)) * nw_ref[...]
        xn_ref[rows] = xn.reshape(half, xn_ref.shape[1], LANES)
        xh = xn.astype(BF16)
        xl = (xn - xh.astype(F32)).astype(BF16)
        both = jnp.dot(xh, rw_ref[...], preferred_element_type=F32)
        logits = (both[:, :LANES] + both[:, LANES:] + jnp.dot(xl, wh, preferred_element_type=F32)) + rb_ref[...]
        r_ref[rows, :] = _route(logits)


def _outproj(x, oa, ob, wa, wb, nw, rw, rb, tm, n_total, row0, prev):
    n, d = x.shape
    blk0 = row0 // tm
    row = lambda i: (i, 0)
    orow = lambda i: (blk0 + i, 0)
    const = lambda i: (0, 0)
    in_specs = [pl.BlockSpec((tm, d), row),
                pl.BlockSpec((tm, oa.shape[1]), row),
                pl.BlockSpec((tm, ob.shape[1]), row),
                pl.BlockSpec(wa.shape, const),
                pl.BlockSpec(wb.shape, const),
                pl.BlockSpec((1, d), const),
                pl.BlockSpec(rw.shape, const),
                pl.BlockSpec((1, LANES), const)]
    args = [x, oa, ob, wa, wb, nw, rw, rb]
    aliases = {}
    if prev is not None:
        in_specs += [pl.BlockSpec(memory_space=pl.ANY)] * 3
        aliases = {len(args) + t: t for t in range(3)}
        args += list(prev)
    return pl.pallas_call(
        _outproj_kernel,
        grid=(n // tm,),
        in_specs=in_specs,
        out_specs=[pl.BlockSpec((tm, d), orow), pl.BlockSpec((tm, d // LANES, LANES), lambda i: (blk0 + i, 0, 0)),
                   pl.BlockSpec((tm, LANES), orow)],
        out_shape=[jax.ShapeDtypeStruct((n_total, d), F32), jax.ShapeDtypeStruct((n_total, d // LANES, LANES), F32),
                   jax.ShapeDtypeStruct((n_total, LANES), F32)],
        input_output_aliases=aliases,
        compiler_params=_params(("arbitrary",)),
        name="outproj",
    )(*args)


def _row_out(x_ref, r, dst_hbm, row, sem):
    return pltpu.make_async_copy(x_ref.at[r], dst_hbm.at[row], sem)


def _zero_fill_copies(z_ref, xs_hbm, start, length, sem):
    pairs = []
    bit = z_ref.shape[0]
    while bit >= 1:
        off = start + (length & (-2 * bit))
        pairs.append(((length & bit) != 0,
                      pltpu.make_async_copy(z_ref.at[pl.ds(0, bit)], xs_hbm.at[pl.ds(off, bit)], sem)))
        bit //= 2
    return pairs


def _scatter_kernel(ps_ref, pl_ref, pos_ref, x_ref, xs_hbm, z_ref, sem, zsem, *, tm):
    @pl.when(pl.program_id(0) == 0)
    def _():
        z_ref[...] = jnp.zeros_like(z_ref)

        def fill(e, carry):
            for pred, cp in _zero_fill_copies(z_ref, xs_hbm, ps_ref[e], pl_ref[e], zsem):
                pl.when(pred)(cp.start)
            return carry
        lax.fori_loop(0, N_EXPERTS, fill, 0)

    @pl.when(pl.program_id(0) == pl.num_programs(0) - 1)
    def _():
        def settle(e, carry):
            for pred, cp in _zero_fill_copies(z_ref, xs_hbm, ps_ref[e], pl_ref[e], zsem):
                pl.when(pred)(cp.wait)
            return carry
        lax.fori_loop(0, N_EXPERTS, settle, 0)

    def issue(r, carry):
        _row_out(x_ref, r, xs_hbm, pos_ref[0, 0, 2 * r], sem).start()
        _row_out(x_ref, r, xs_hbm, pos_ref[0, 0, 2 * r + 1], sem).start()
        return carry
    lax.fori_loop(0, tm, issue, 0, unroll=8)

    def drain(r, carry):
        _row_out(x_ref, r, xs_hbm, 0, sem).wait()
        _row_out(x_ref, r, xs_hbm, 0, sem).wait()
        return carry
    lax.fori_loop(0, tm, drain, 0, unroll=8)


def _scatter(pad_start, pad_len, pos, x, tm, rows, tm_rows):
    n, g, _ = x.shape
    return pl.pallas_call(
        functools.partial(_scatter_kernel, tm=tm),
        grid_spec=pltpu.PrefetchScalarGridSpec(
            num_scalar_prefetch=2,
            grid=(n // tm,),
            in_specs=[pl.BlockSpec((1, 1, 2 * tm), lambda m, *_: (m, 0, 0), memory_space=pltpu.SMEM),
                      pl.BlockSpec((tm, g, LANES), lambda m, *_: (m, 0, 0))],
            out_specs=pl.BlockSpec(memory_space=pl.ANY),
            scratch_shapes=[pltpu.VMEM((tm_rows // 2, g, LANES), x.dtype), pltpu.SemaphoreType.DMA(()),
                            pltpu.SemaphoreType.DMA(())]),
        out_shape=jax.ShapeDtypeStruct((rows, g, LANES), x.dtype),
        compiler_params=_params(("arbitrary",)),
        name="scatter",
    )(pad_start, pad_len, pos, x)


def _weight_copies(e, slot, w_hbm, w_stage, sem):
    return [pltpu.make_async_copy(w_hbm[t].at[e], w_stage[t].at[slot], sem.at[slot, t]) for t in range(3)]


def _experts_kernel(te_ref, nu_ref, nx_ref, ig_ref, x_ref, wg_hbm, wu_hbm, wd_hbm, o_ref,
                    sg, su, sd, g_scr, u_scr, d_scr, sem):
    m = pl.program_id(0)
    w_hbm, w_stage = (wg_hbm, wu_hbm, wd_hbm), (sg, su, sd)

    @pl.when(m < nu_ref[0])
    def _():
        e = te_ref[m]
        new_expert = (m == 0) | (e != te_ref[jnp.maximum(m - 1, 0)])
        slot = ig_ref[m] % 2

        @pl.when(m == 0)
        def _():
            for cp in _weight_copies(e, slot, w_hbm, w_stage, sem):
                cp.start()

        @pl.when(new_expert)
        def _():
            for cp in _weight_copies(e, slot, w_hbm, w_stage, sem):
                cp.wait()

            @pl.when(nx_ref[m] != e)
            def _():
                for cp in _weight_copies(nx_ref[m], 1 - slot, w_hbm, w_stage, sem):
                    cp.start()

            g_scr[...] = sg[slot].astype(BF16)
            u_scr[...] = su[slot].astype(BF16)
            d_scr[...] = sd[slot].astype(BF16)

        x = x_ref[...].reshape(x_ref.shape[0], g_scr.shape[0]).astype(BF16)
        g = jnp.dot(x, g_scr[...], preferred_element_type=F32)
        u = jnp.dot(x, u_scr[...], preferred_element_type=F32)
        hid = ((g * _sigmoid(g)) * u).astype(BF16)
        o_ref[...] = jnp.dot(hid, d_scr[...], preferred_element_type=F32)


def _experts(tile_expert, n_used, tile_next, tile_group, xs, wg, wu, wd, tm):
    rows, g, _ = xs.shape
    d, ff = wg.shape[1], wg.shape[2]
    last = lambda m, nu: jnp.minimum(m, jnp.maximum(nu[0] - 1, 0))
    hbm = pl.BlockSpec(memory_space=pl.ANY)
    return pl.pallas_call(
        _experts_kernel,
        grid_spec=pltpu.PrefetchScalarGridSpec(
            num_scalar_prefetch=4,
            grid=(rows // tm,),
            in_specs=[pl.BlockSpec((tm, g, LANES), lambda m, te, nu, *_: (last(m, nu), 0, 0)), hbm, hbm, hbm],
            out_specs=pl.BlockSpec((tm, d), lambda m, te, nu, *_: (last(m, nu), 0)),
            scratch_shapes=[pltpu.VMEM((2, d, ff), F32), pltpu.VMEM((2, d, ff), F32), pltpu.VMEM((2, ff, d), F32),
                            pltpu.VMEM((d, ff), BF16), pltpu.VMEM((d, ff), BF16), pltpu.VMEM((ff, d), BF16),
                            pltpu.SemaphoreType.DMA((2, 3))]),
        out_shape=jax.ShapeDtypeStruct((rows, d), F32),
        compiler_params=_params(("arbitrary",)),
        name="experts",
    )(tile_expert, n_used, tile_next, tile_group, xs, wg, wu, wd)


def _row_in(src_hbm, row, dst, r, sem):
    return pltpu.make_async_copy(src_hbm.at[pl.ds(row, 1), :], dst.at[pl.ds(r, 1), :], sem)


def _combine_kernel(pos_ref, nxt_ref, x_ref, r_ref, fw_ref, ys_hbm, o_ref, buf, sem, *, tm, ntiles):
    m = pl.program_id(0)
    slot = m % 2

    def issue(tbl, s):
        def go(r, carry):
            _row_in(ys_hbm, tbl[0, 0, 2 * r], buf.at[s, 0], r, sem.at[s]).start()
            _row_in(ys_hbm, tbl[0, 0, 2 * r + 1], buf.at[s, 1], r, sem.at[s]).start()
            return carry
        lax.fori_loop(0, tm, go, 0, unroll=8)

    @pl.when(m == 0)
    def _():
        issue(pos_ref, 0)

    @pl.when(m + 1 < ntiles)
    def _():
        issue(nxt_ref, 1 - slot)

    def drain(r, carry):
        _row_in(ys_hbm, 0, buf.at[slot, 0], r, sem.at[slot]).wait()
        _row_in(ys_hbm, 0, buf.at[slot, 1], r, sem.at[slot]).wait()
        return carry
    lax.fori_loop(0, tm, drain, 0, unroll=8)
    rt = r_ref[...]
    x = x_ref[...] + (buf[slot, 0] * rt[:, 2:3] + buf[slot, 1] * rt[:, 3:4])
    ms = jnp.mean(x * x, axis=-1, keepdims=True)
    o_ref[...] = (x * lax.rsqrt(ms + NORM_EPS)) * fw_ref[...]


def _combine(pos, x2, r, fw, ys, tm, row0, n):
    d = x2.shape[1]
    blk0 = row0 // tm
    ntiles = n // tm
    smem = functools.partial(pl.BlockSpec, memory_space=pltpu.SMEM)
    return pl.pallas_call(
        functools.partial(_combine_kernel, tm=tm, ntiles=ntiles),
        grid=(ntiles,),
        in_specs=[smem((1, 1, 2 * tm), lambda m: (blk0 + m, 0, 0)),
                  smem((1, 1, 2 * tm), lambda m: (blk0 + jnp.minimum(m + 1, ntiles - 1), 0, 0)),
                  pl.BlockSpec((tm, d), lambda m: (blk0 + m, 0)),
                  pl.BlockSpec((tm, LANES), lambda m: (blk0 + m, 0)),
                  pl.BlockSpec((1, d), lambda m: (0, 0)),
                  pl.BlockSpec(memory_space=pl.ANY)],
        out_specs=pl.BlockSpec((tm, d), lambda m: (m, 0)),
        out_shape=jax.ShapeDtypeStruct((n, d), F32),
        scratch_shapes=[pltpu.VMEM((2, 2, tm, d), F32), pltpu.SemaphoreType.DMA((2,))],
        compiler_params=_params(("arbitrary",)),
        name="combine",
    )(pos, pos, x2, r, fw, ys)


def _dispatch_tables(r, tm):
    n = r.shape[0]
    tiles = (2 * n) // tm + N_EXPERTS
    ids = jnp.arange(N_EXPERTS, dtype=jnp.int32)
    e = r[:, 0:2].astype(jnp.int32).reshape(-1)
    onehot = (e[:, None] == ids[None, :]).astype(jnp.int32)
    incl = jnp.cumsum(onehot, axis=0)
    counts = incl[-1]
    etiles = (counts + tm - 1) // tm
    tile_end = jnp.cumsum(etiles)
    tile_start = tile_end - etiles
    n_used = tile_end[-1]
    pos = jnp.sum(onehot * (incl - 1 + (tile_start * tm)[None, :]), axis=1)
    m = jnp.minimum(jnp.arange(tiles, dtype=jnp.int32), n_used - 1)
    te = jnp.sum((tile_end[None, :] <= m[:, None]).astype(jnp.int32), axis=1)
    sel = (te[:, None] == ids[None, :]).astype(jnp.int32)
    pick = lambda tbl: jnp.sum(sel * tbl[None, :], axis=1)
    used = counts > 0
    later = used[None, :] & (ids[None, :] > ids[:, None])
    next_used = jnp.min(jnp.where(later, ids[None, :], N_EXPERTS), axis=1)
    next_used = jnp.where(next_used == N_EXPERTS, ids, next_used)
    group = jnp.cumsum(used.astype(jnp.int32)) - 1
    i32 = lambda a: a.astype(jnp.int32)
    return (i32(pos), i32(te), i32(n_used).reshape(1), i32(pick(next_used)), i32(pick(group)),
            i32(tile_start * tm + counts), i32(etiles * tm - counts))


_TILES = dict(
    token_tile=256,
    expert_row_tile=256,
    inproj_column_blocks=2,
    hgrn_heads=4,
    hgrn_time_block=1024,
    sb_block=256,
    sb_heads=8,
    decode_key_block=512,
    decode_dma_block=2048,
)

def kernel(x_prompt, x_sample, cache_sb_k, cache_sb_v, state_hgrn, w_in, hg_lb_logits, hg_norm_w, w_out, norm1_w,
           norm2_w, router_group_w, router_group_b, router_expert_w, router_expert_b, expert_w_gate, expert_w_up,
           expert_w_down, final_norm_w):
    bp, tp, d = x_prompt.shape
    bs, ts, _ = x_sample.shape
    np_, ns = bp * tp, bs * ts
    n = np_ + ns
    heads = w_in.shape[2] // (7 * HEAD_DIM)
    hw = heads * HEAD_DIM
    cfg = _TILES
    tm_tok, tm_moe = cfg["token_tile"], cfg["expert_row_tile"]

    w_out_b = w_out[0].astype(BF16)
    n1 = norm1_w[0].reshape(1, d)
    n2 = norm2_w[0].reshape(1, d)
    hnw = hg_norm_w[0].reshape(1, HEAD_DIM)
    fw = final_norm_w.reshape(1, d)

    xp = x_prompt.reshape(np_, d)
    xs = x_sample.reshape(ns, d)
    w_in_b = w_in[0].astype(BF16)
    w_cols = w_in_b.shape[1] // cfg["inproj_column_blocks"]
    proj_p = _inproj(xp, n1, w_in_b, tm_tok, w_cols)
    proj_s = _inproj(xs, n1, w_in_b, min(ns, tm_tok), w_cols)

    ohg_p, st_p = _hgrn(proj_p, hg_lb_logits, hnw, None, bp, tp, heads, 0, cfg["hgrn_heads"],
                        min(tp, cfg["hgrn_time_block"]))
    ohg_s, st_s = _hgrn(proj_s, hg_lb_logits, hnw, state_hgrn[0], bs, ts, heads, 0, heads, ts)
    osb_p, k_p, v_p = _sb_prompt(proj_p, bp, tp, heads, 4 * heads, cfg["sb_block"], cfg["sb_heads"])
    osb_s = _sb_decode(proj_s, cache_sb_k[0], cache_sb_v[0], bs, ts, heads, 4 * heads,
                       cfg["decode_key_block"], cfg["decode_dma_block"])

    rw = jnp.concatenate([router_group_w[0], router_expert_w[0]], axis=1)
    rw = jnp.pad(rw, ((0, 0), (0, LANES - rw.shape[1])))
    rwh = rw.astype(BF16)
    rw2 = jnp.concatenate([rwh, (rw - rwh.astype(F32)).astype(BF16)], axis=1)
    rb = jnp.concatenate([router_group_b[0], router_expert_b[0]])
    rb = jnp.pad(rb, (0, LANES - rb.shape[0])).reshape(1, LANES)

    wa, wb = w_out_b[:hw], w_out_b[hw:]
    outs = _outproj(xp, ohg_p, osb_p, wa, wb, n2, rw2, rb, 2 * tm_tok, n, 0, None)
    x2, xn2, r = _outproj(xs, ohg_s, osb_s, wa, wb, n2, rw2, rb, tm_tok, n, np_, outs)

    pos, tile_expert, n_used, tile_next, tile_group, pad_start, pad_len = _dispatch_tables(r, tm_moe)
    pos = pos.reshape(n // tm_tok, 1, 2 * tm_tok)
    xg = _scatter(pad_start, pad_len, pos, xn2, tm_tok, tile_expert.shape[0] * tm_moe, tm_moe)
    ys = _experts(tile_expert, n_used, tile_next, tile_group, xg,
                  expert_w_gate[0], expert_w_up[0], expert_w_down[0], tm_moe)
    y_p = _combine(pos, x2, r, fw, ys, tm_tok, 0, np_)
    y_s = _combine(pos, x2, r, fw, ys, tm_tok, np_, ns)

    heads_s = lambda a: a.reshape(bs, ts, heads, HEAD_DIM).transpose(0, 2, 1, 3)[None]
    k_s = heads_s(proj_s[:, 5 * hw:6 * hw])
    v_s = heads_s(proj_s[:, 6 * hw:7 * hw])
    return (y_p.reshape(bp, tp, d), y_s.reshape(bs, ts, d), k_p[None], v_p[None], st_p[None],
            k_s, v_s, st_s[None])
```

```python
import functools

import jax
import jax.numpy as jnp
import numpy as np
from jax import lax
from jax.experimental import pallas as pl
from jax.experimental.pallas import tpu as pltpu

F32 = jnp.float32
BF16 = jnp.bfloat16
NORM_EPS = 1e-6
LANES = 128
HEAD_DIM = 128
HGRN_CHUNK = 64
HGRN_SUB = 16
N_GROUPS = 4
EXPERTS_PER_GROUP = 8
N_EXPERTS = N_GROUPS * EXPERTS_PER_GROUP
VMEM_LIMIT = 48 * 1024 * 1024
NEG_BIG = -1e30
LOG2E = 1.4426950408889634

_NT = (((1,), (1,)), ((), ()))
_TN = (((0,), (0,)), ((), ()))


def _params(sem):
    return pltpu.CompilerParams(dimension_semantics=sem, vmem_limit_bytes=VMEM_LIMIT)


def _sigmoid(x):
    return 1.0 / (1.0 + jnp.exp(-x))


def _split3(x):
    h = x.astype(BF16)
    r = x - h.astype(F32)
    m = r.astype(BF16)
    l = (r - m.astype(F32)).astype(BF16)
    return h, m, l


def _cat(parts, axis):
    return parts[0] if len(parts) == 1 else jnp.concatenate(parts, axis=axis)


def _inproj_kernel(xa_ref, xb_ref, nw_ref, w_ref, o_ref, *, na):
    def project(x_ref):
        x = x_ref[...]
        ms = jnp.mean(x * x, axis=-1, keepdims=True)
        xn = ((x * lax.rsqrt(ms + NORM_EPS)) * nw_ref[...]).astype(BF16)
        o_ref[...] = jnp.dot(xn, w_ref[...], preferred_element_type=F32)

    i = pl.program_id(1)
    pl.when(i < na)(lambda: project(xa_ref))
    pl.when(i >= na)(lambda: project(xb_ref))


def _inproj(xa, xb, nw, w, tm, tn):
    d = xa.shape[1]
    c = w.shape[1]
    na, nb = xa.shape[0] // tm, xb.shape[0] // tm
    return pl.pallas_call(
        functools.partial(_inproj_kernel, na=na),
        grid=(c // tn, na + nb),
        in_specs=[pl.BlockSpec((tm, d), lambda j, i: (jnp.minimum(i, na - 1), 0)),
                  pl.BlockSpec((tm, d), lambda j, i: (jnp.maximum(i - na, 0), 0)),
                  pl.BlockSpec((1, d), lambda j, i: (0, 0)),
                  pl.BlockSpec((d, tn), lambda j, i: (0, j))],
        out_specs=pl.BlockSpec((tm, tn), lambda j, i: (i, j)),
        out_shape=jax.ShapeDtypeStruct(((na + nb) * tm, c), F32),
        compiler_params=_params(("arbitrary", "arbitrary")),
        name="inproj",
    )(xa, xb, nw, w)


def _hgrn_kernel(*refs, tb, chunk, sub, hp, has_state):
    if has_state:
        (hq_ref, hf_ref, hi_ref, hg_ref, lbl_ref, nw_ref, tri_ref, wsel_ref, s0_ref,
         o_ref, sout_ref, st_ref, q_scr, b_scr, c_scr) = refs
    else:
        (hq_ref, hf_ref, hi_ref, hg_ref, lbl_ref, nw_ref, tri_ref, wsel_ref,
         o_ref, sout_ref, st_ref, q_scr, b_scr, c_scr) = refs
    n_sub = chunk // sub
    ti = pl.program_id(2)
    hs = [slice(h * HEAD_DIM, (h + 1) * HEAD_DIM) for h in range(hp)]

    lg = lbl_ref[...]
    lg = jnp.exp(lg - jnp.max(lg, axis=0, keepdims=True))
    lb = lg[0:1, :] / jnp.sum(lg, axis=0, keepdims=True)

    @pl.when(ti == 0)
    def _():
        for h in range(hp):
            st_ref[h] = s0_ref[0, h].T if has_state else jnp.zeros((HEAD_DIM, HEAD_DIM), F32)

    row = lax.broadcasted_iota(jnp.int32, (chunk, chunk), 0)
    col = lax.broadcasted_iota(jnp.int32, (chunk, chunk), 1)
    blk_diff = row // sub - col // sub
    diag_mask = (blk_diff == 0) & (col <= row)
    diag_mask_all = _cat([diag_mask] * hp, 0)
    tri = tri_ref[...]
    wsel = wsel_ref[...]
    nw = nw_ref[...]

    def prepare(ci, slot):
        r0 = pl.multiple_of(ci * chunk, chunk)
        hq = hq_ref[pl.ds(r0, chunk), :]
        hf = hf_ref[pl.ds(r0, chunk), :]
        f = lb + (1.0 - lb) * _sigmoid(hf)
        k = 1.0 - f
        l1, l2, l3 = _split3(jnp.log(f) * LOG2E)
        b = (jnp.dot(tri, l1, preferred_element_type=F32) + jnp.dot(tri, l2, preferred_element_type=F32)
             + jnp.dot(tri, l3, preferred_element_type=F32))
        q_scr[slot] = hq * _sigmoid(hq)
        b_scr[slot] = b
        c_scr[slot] = b - jnp.log(k) * LOG2E

    def main(ci, slot):
        r0 = pl.multiple_of(ci * chunk, chunk)
        v = hi_ref[pl.ds(r0, chunk), :]
        hg = hg_ref[pl.ds(r0, chunk), :]
        q = q_scr[slot]
        b = b_scr[slot]
        c = c_scr[slot]
        b_rows = b_scr.at[slot]
        c_rows = c_scr.at[slot]
        vb = v.astype(BF16)
        b_last = b_rows[pl.ds(chunk - 1, 1), :]
        qd = (q * jnp.exp2(b)).astype(BF16)
        kd = jnp.exp2(b_last - c).astype(BF16)
        s_decay = jnp.exp2(b_last)

        slabs = []
        for r in range(sub):
            parts = []
            for i in range(n_sub):
                c_row = c_rows[pl.ds(i * sub + r, 1), :]
                sl = slice(i * sub, (i + 1) * sub)
                parts.append(q[sl] * jnp.exp2(jnp.minimum(b[sl] - c_row, 0.0)))
            slabs.append(_cat(parts, 0).astype(BF16))
        xcat = _cat([_cat([s[:, hs[h]] for s in slabs], 1) for h in range(hp)], 0)
        a_all = jnp.where(diag_mask_all, jnp.dot(xcat, wsel, preferred_element_type=F32), 0.0)

        if n_sub > 1:
            starts = [None] + [b_rows[pl.ds(i * sub - 1, 1), :] for i in range(1, n_sub)]
            ends = [b_rows[pl.ds(i * sub + sub - 1, 1), :] for i in range(n_sub)]
            qh, kh = [], []
            for i in range(n_sub):
                sl = slice(i * sub, (i + 1) * sub)
                qh.append(q[sl] if i == 0 else q[sl] * jnp.exp2(b[sl] - starts[i]))
                kh.append(jnp.exp2(ends[i] - c[sl]))
            khat = _cat(kh, 0).astype(BF16)
            qds = []
            for d in range(1, n_sub):
                parts = []
                for i in range(n_sub):
                    if i < d:
                        parts.append(jnp.zeros((sub, hp * HEAD_DIM), F32))
                    elif d == 1:
                        parts.append(qh[i])
                    else:
                        parts.append(qh[i] * jnp.exp2(starts[i] - ends[i - d]))
                qds.append(_cat(parts, 0).astype(BF16))
            qstack = _cat(qds, 0)

        outs = []
        for h in range(hp):
            st = st_ref[h]
            o = lax.dot_general(qd[:, hs[h]], st.astype(BF16), _NT, preferred_element_type=F32)
            st_ref[h] = st * s_decay[:, hs[h]] + lax.dot_general(vb[:, hs[h]], kd[:, hs[h]], _TN,
                                                                 preferred_element_type=F32)
            a = a_all[h * chunk:(h + 1) * chunk]
            if n_sub > 1:
                m = lax.dot_general(qstack[:, hs[h]], khat[:, hs[h]], _NT, preferred_element_type=F32)
                for d in range(1, n_sub):
                    a = a + jnp.where(blk_diff == d, m[(d - 1) * chunk:d * chunk], 0.0)
            o = o + jnp.dot(a.astype(BF16), vb[:, hs[h]], preferred_element_type=F32)
            ms = jnp.mean(o * o, axis=-1, keepdims=True)
            outs.append((o * lax.rsqrt(ms + NORM_EPS)) * nw[:, hs[h]])
        g = hg * _sigmoid(hg)
        o_ref[pl.ds(r0, chunk), :] = (_cat(outs, 1) * g).astype(o_ref.dtype)

    n_chunks = tb // chunk
    prepare(0, 0)
    if n_chunks == 1:
        main(0, 0)
    else:
        def pair(pp, carry):
            ci = 2 * pp
            prepare(ci + 1, 1)
            main(ci, 0)
            prepare(jnp.minimum(ci + 2, n_chunks - 1), 0)
            main(ci + 1, 1)
            return carry

        lax.fori_loop(0, n_chunks // 2, pair, 0)

    @pl.when(ti == pl.num_programs(2) - 1)
    def _():
        for h in range(hp):
            sout_ref[0, h] = st_ref[h].T


def _hgrn_consts(chunk, sub):
    t = np.arange(chunk)
    tri = (t[None, :] <= t[:, None]).astype(np.float32)
    rows = np.arange(sub * HEAD_DIM) // HEAD_DIM
    wsel = (rows[:, None] == (t[None, :] % sub)).astype(np.float32)
    return jnp.asarray(tri, BF16), jnp.asarray(wsel, BF16)


def _hgrn(proj, row0, lb_logits, norm_w, state, batch, seq, heads, col0, hp, tb):
    chunk = min(HGRN_CHUNK, seq)
    sub = min(HGRN_SUB, chunk)
    tri, wsel = _hgrn_consts(chunk, sub)
    has_state = state is not None
    nt = seq // tb
    w = hp * HEAD_DIM

    def col_spec(g):
        return pl.BlockSpec((tb, w), lambda b, h, t: (row0 // tb + b * nt + t, (col0 + g * heads) // hp + h))

    const = lambda b, h, t: (0, 0)
    in_specs = [col_spec(0), col_spec(1), col_spec(2), col_spec(3),
                pl.BlockSpec((lb_logits.shape[0], w), lambda b, h, t: (0, h)),
                pl.BlockSpec((1, w), const),
                pl.BlockSpec(tri.shape, const),
                pl.BlockSpec(wsel.shape, const)]
    args = [proj, proj, proj, proj, lb_logits, jnp.tile(norm_w, (1, hp)), tri, wsel]
    if has_state:
        in_specs.append(pl.BlockSpec((1, hp, HEAD_DIM, HEAD_DIM), lambda b, h, t: (b, h, 0, 0)))
        args.append(state)
    return pl.pallas_call(
        functools.partial(_hgrn_kernel, tb=tb, chunk=chunk, sub=sub, hp=hp, has_state=has_state),
        grid=(batch, heads // hp, nt),
        in_specs=in_specs,
        out_specs=[pl.BlockSpec((tb, w), lambda b, h, t: (b * nt + t, h)),
                   pl.BlockSpec((1, hp, HEAD_DIM, HEAD_DIM), lambda b, h, t: (b, h, 0, 0))],
        out_shape=[jax.ShapeDtypeStruct((batch * seq, heads * HEAD_DIM), BF16),
                   jax.ShapeDtypeStruct((batch, heads, HEAD_DIM, HEAD_DIM), F32)],
        scratch_shapes=[pltpu.VMEM((hp, HEAD_DIM, HEAD_DIM), F32),
                        pltpu.VMEM((2, chunk, w), F32),
                        pltpu.VMEM((2, chunk, w), F32),
                        pltpu.VMEM((2, chunk, w), F32)],
        compiler_params=_params(("arbitrary", "arbitrary", "arbitrary")),
        name="hgrn_state" if has_state else "hgrn",
    )(*args)


def _sb_blocks(qs, ks, vs, u, c, mask):
    tq = qs[0].shape[0]
    z = _cat([lax.dot_general(q, k, _NT, preferred_element_type=F32) for q, k in zip(qs, ks)], 0)
    nz = -z
    sp = jnp.maximum(nz, 0.0) + jnp.log2(1.0 + jnp.exp2(jnp.minimum(z, nz)))
    l = nz - sp
    if mask is not None:
        l = jnp.where(mask, l, 0.0)
    s = jnp.dot(l.astype(BF16), u, preferred_element_type=F32)
    p = jnp.exp2(s + c - sp)
    if mask is not None:
        p = jnp.where(mask, p, 0.0)
    p = p.astype(BF16)
    pv = _cat([jnp.dot(p[h * tq:(h + 1) * tq], vs[h], preferred_element_type=F32) for h in range(len(qs))], 0)
    return pv, c + (s[:, 0:1] + l[:, 0:1])


def _sb_suffix(tk):
    s = np.arange(tk)
    u = (s[:, None] > s[None, :]).astype(np.float32)
    return jnp.asarray(u, BF16)


def _sb_prompt_kernel(q_ref, k_ref, v_ref, u_ref, o_ref, ko_ref, vo_ref, acc_ref, c_ref, kb_ref, vb_ref,
                      lc_ref, sp_ref, l0_ref, *, tq, hp, scale):
    i = pl.program_id(2)
    hs = [slice(h * HEAD_DIM, (h + 1) * HEAD_DIM) for h in range(hp)]

    for h in range(hp):
        ko_ref[0, h] = k_ref[:, hs[h]]
        vo_ref[0, h] = v_ref[:, hs[h]]
    own = pl.ds(pl.multiple_of(i * tq, tq), tq)
    kb_ref[own, :] = k_ref[...].astype(BF16)
    vb_ref[own, :] = v_ref[...].astype(BF16)

    q = (q_ref[...] * scale).astype(BF16)
    usuf = u_ref[...]

    def scores(j, slot, diagonal):
        kb = kb_ref[pl.ds(pl.multiple_of(j * tq, tq), tq), :]
        z = _cat([lax.dot_general(q[:, hs[h]], kb[:, hs[h]], _NT, preferred_element_type=F32) for h in range(hp)], 0)
        nz = -z
        sp = jnp.maximum(nz, 0.0) + jnp.log2(1.0 + jnp.exp2(jnp.minimum(z, nz)))
        l = nz - sp
        if diagonal:
            row = lax.broadcasted_iota(jnp.int32, (tq, tq), 0)
            col = lax.broadcasted_iota(jnp.int32, (tq, tq), 1)
            mask = _cat([col < row] * hp, 0)
            l = jnp.where(mask, l, 0.0)
            sp = jnp.where(mask, sp, -NEG_BIG)
        lc_ref[slot] = l.astype(BF16)
        sp_ref[slot] = sp
        l0_ref[slot] = l[:, 0:1]

    def values(j, slot):
        vb = vb_ref[pl.ds(pl.multiple_of(j * tq, tq), tq), :]
        s = jnp.dot(lc_ref[slot], usuf, preferred_element_type=F32)
        c = c_ref[...]
        p = jnp.exp2(s + c - sp_ref[slot]).astype(BF16)
        acc_ref[...] += _cat([jnp.dot(p[h * tq:(h + 1) * tq], vb[:, hs[h]], preferred_element_type=F32)
                              for h in range(hp)], 0)
        c_ref[...] = c + (s[:, 0:1] + l0_ref[slot])

    acc_ref[...] = jnp.zeros_like(acc_ref)
    c_ref[...] = jnp.zeros_like(c_ref)
    scores(i, 0, True)

    def pair(pp, carry):
        j = i - 1 - 2 * pp
        scores(j, 1, False)
        values(j + 1, 0)
        scores(j - 1, 0, False)
        values(j, 1)
        return carry

    lax.fori_loop(0, i // 2, pair, 0)

    @pl.when(i % 2 == 1)
    def _():
        scores(0, 1, False)
        values(1, 0)
        values(0, 1)

    @pl.when(i % 2 == 0)
    def _():
        values(0, 0)
    o_ref[...] = _cat([acc_ref[pl.ds(h * tq, tq), :] for h in range(hp)], 1).astype(o_ref.dtype)


def _sb_prompt(proj, batch, seq, heads, col0, tq, hp):
    nq = seq // tq
    w = hp * HEAD_DIM
    usuf = _sb_suffix(tq)
    kv_shape = jax.ShapeDtypeStruct((batch, heads, seq, HEAD_DIM), F32)
    return pl.pallas_call(
        functools.partial(_sb_prompt_kernel, tq=tq, hp=hp, scale=HEAD_DIM ** -0.5 * LOG2E),
        grid=(batch, heads // hp, nq),
        in_specs=[pl.BlockSpec((tq, w), lambda b, h, i: (b * nq + i, col0 // hp + h)),
                  pl.BlockSpec((tq, w), lambda b, h, i: (b * nq + i, (col0 + heads) // hp + h)),
                  pl.BlockSpec((tq, w), lambda b, h, i: (b * nq + i, (col0 + 2 * heads) // hp + h)),
                  pl.BlockSpec(usuf.shape, lambda b, h, i: (0, 0))],
        out_specs=[pl.BlockSpec((tq, w), lambda b, h, i: (b * nq + i, h)),
                   pl.BlockSpec((1, hp, tq, HEAD_DIM), lambda b, h, i: (b, h, i, 0)),
                   pl.BlockSpec((1, hp, tq, HEAD_DIM), lambda b, h, i: (b, h, i, 0))],
        out_shape=[jax.ShapeDtypeStruct((batch * seq, heads * HEAD_DIM), BF16), kv_shape, kv_shape],
        scratch_shapes=[pltpu.VMEM((hp * tq, HEAD_DIM), F32), pltpu.VMEM((hp * tq, 1), F32),
                        pltpu.VMEM((seq, w), BF16), pltpu.VMEM((seq, w), BF16),
                        pltpu.VMEM((2, hp * tq, tq), BF16), pltpu.VMEM((2, hp * tq, tq), F32),
                        pltpu.VMEM((2, hp * tq, 1), F32)],
        compiler_params=_params(("arbitrary", "arbitrary", "arbitrary")),
        name="sb_prompt",
    )(proj, proj, proj, usuf)


def _sb_decode_kernel(q_ref, k_ref, v_ref, pk_ref, pv_ref, un_ref, up_ref, o_ref, acc_ref, c_ref,
                      *, seq, heads, tk, scale):
    j = pl.program_id(1)
    hs = [slice(h * HEAD_DIM, (h + 1) * HEAD_DIM) for h in range(heads)]
    q = (q_ref[...] * scale).astype(BF16)
    qs = [q[:, hs[h]] for h in range(heads)]

    @pl.when(j == 0)
    def _():
        pad = jnp.zeros((LANES - seq, heads * HEAD_DIM), F32)
        kn = jnp.concatenate([k_ref[...], pad], axis=0).astype(BF16)
        vn = jnp.concatenate([v_ref[...], pad], axis=0).astype(BF16)
        row = lax.broadcasted_iota(jnp.int32, (seq, LANES), 0)
        col = lax.broadcasted_iota(jnp.int32, (seq, LANES), 1)
        mask = _cat([col < row] * heads, 0)
        pv, c = _sb_blocks(qs, [kn[:, hs[h]] for h in range(heads)], [vn[:, hs[h]] for h in range(heads)],
                           un_ref[...], jnp.zeros((heads * seq, 1), F32), mask)
        acc_ref[...] = pv
        c_ref[...] = c

    usuf = up_ref[...]
    for sblk in reversed(range(pk_ref.shape[2] // tk)):
        ks = [pk_ref[0, h, pl.ds(sblk * tk, tk), :].astype(BF16) for h in range(heads)]
        vs = [pv_ref[0, h, pl.ds(sblk * tk, tk), :].astype(BF16) for h in range(heads)]
        pv, c = _sb_blocks(qs, ks, vs, usuf, c_ref[...], None)
        acc_ref[...] += pv
        c_ref[...] = c

    @pl.when(j == pl.num_programs(1) - 1)
    def _():
        o_ref[...] = _cat([acc_ref[pl.ds(h * seq, seq), :] for h in range(heads)], 1).astype(o_ref.dtype)


def _sb_decode(proj, row0, past_k, past_v, batch, seq, heads, col0, tk, tkb):
    past = past_k.shape[2]
    nblk = past // tkb
    w = heads * HEAD_DIM
    un, up = _sb_suffix(LANES), _sb_suffix(tk)
    kv_spec = pl.BlockSpec((1, heads, tkb, HEAD_DIM), lambda b, j: (b, 0, nblk - 1 - j, 0))
    return pl.pallas_call(
        functools.partial(_sb_decode_kernel, seq=seq, heads=heads, tk=tk, scale=HEAD_DIM ** -0.5 * LOG2E),
        grid=(batch, nblk),
        in_specs=[pl.BlockSpec((seq, w), lambda b, j: (row0 // seq + b, col0 // heads)),
                  pl.BlockSpec((seq, w), lambda b, j: (row0 // seq + b, col0 // heads + 1)),
                  pl.BlockSpec((seq, w), lambda b, j: (row0 // seq + b, col0 // heads + 2)),
                  kv_spec, kv_spec,
                  pl.BlockSpec(un.shape, lambda b, j: (0, 0)),
                  pl.BlockSpec(up.shape, lambda b, j: (0, 0))],
        out_specs=pl.BlockSpec((seq, w), lambda b, j: (b, 0)),
        out_shape=jax.ShapeDtypeStruct((batch * seq, w), BF16),
        scratch_shapes=[pltpu.VMEM((heads * seq, HEAD_DIM), F32), pltpu.VMEM((heads * seq, 1), F32)],
        compiler_params=_params(("arbitrary", "arbitrary")),
        name="sb_decode",
    )(proj, proj, proj, past_k, past_v, un, up)


def _route(logits):
    lane = lax.broadcasted_iota(jnp.int32, logits.shape, 1).astype(F32)
    first = lambda hit: jnp.min(jnp.where(hit, lane, float(LANES)), axis=-1, keepdims=True)
    is_g = lane < N_GROUPS
    gl = jnp.where(is_g, logits, NEG_BIG)
    gmax = jnp.max(gl, axis=-1, keepdims=True)
    gidx = first(gl == gmax)
    gsum = jnp.sum(jnp.where(is_g, jnp.exp(gl - gmax), 0.0), axis=-1, keepdims=True)
    gw = 1.0 / gsum
    lo = N_GROUPS + EXPERTS_PER_GROUP * gidx
    in_g = (lane >= lo) & (lane < lo + EXPERTS_PER_GROUP)
    el = jnp.where(in_g, logits, NEG_BIG)
    m1 = jnp.max(el, axis=-1, keepdims=True)
    i1 = first(in_g & (el == m1))
    rest = in_g & (lane != i1)
    el2 = jnp.where(rest, logits, NEG_BIG)
    m2 = jnp.max(el2, axis=-1, keepdims=True)
    i2 = first(rest & (el2 == m2))
    t = jnp.exp(m2 - m1)
    w0 = gw / (1.0 + t)
    w1 = gw * t / (1.0 + t)
    e0 = i1 - N_GROUPS
    e1 = i2 - N_GROUPS
    return jnp.where(lane == 0, e0, jnp.where(lane == 1, e1, jnp.where(lane == 2, w0, jnp.where(lane == 3, w1, 0.0))))


def _outproj_kernel(x_ref, a_ref, b_ref, wa_ref, wb_ref, nw_ref, rw_ref, rb_ref, *rest):
    x2_ref, xn_ref, r_ref = rest[-3:]
    wh = rw_ref[:, :LANES]
    half = min(x_ref.shape[0], 256)
    for h0 in range(0, x_ref.shape[0], half):
        rows = pl.ds(h0, half)
        acc = jnp.dot(a_ref[rows, :], wa_ref[...], preferred_element_type=F32)
        acc = acc + jnp.dot(b_ref[rows, :], wb_ref[...], preferred_element_type=F32)
        x2 = x_ref[rows, :] + acc
        x2_ref[rows, :] = x2
        ms = jnp.mean(x2 * x2, axis=-1, keepdims=True)
        xn = (x2 * lax.rsqrt(ms + NORM_EPS)) * nw_ref[...]
        xn_ref[rows, :] = xn
        xh = xn.astype(BF16)
        xl = (xn - xh.astype(F32)).astype(BF16)
        both = jnp.dot(xh, rw_ref[...], preferred_element_type=F32)
        logits = (both[:, :LANES] + both[:, LANES:] + jnp.dot(xl, wh, preferred_element_type=F32)) + rb_ref[...]
        r_ref[rows, :] = _route(logits)


def _outproj(x, oa, ob, wa, wb, nw, rw, rb, tm, n_total, row0, prev):
    n, d = x.shape
    blk0 = row0 // tm
    row = lambda i: (i, 0)
    orow = lambda i: (blk0 + i, 0)
    const = lambda i: (0, 0)
    in_specs = [pl.BlockSpec((tm, d), row),
                pl.BlockSpec((tm, oa.shape[1]), row),
                pl.BlockSpec((tm, ob.shape[1]), row),
                pl.BlockSpec(wa.shape, const),
                pl.BlockSpec(wb.shape, const),
                pl.BlockSpec((1, d), const),
                pl.BlockSpec(rw.shape, const),
                pl.BlockSpec((1, LANES), const)]
    args = [x, oa, ob, wa, wb, nw, rw, rb]
    aliases = {}
    if prev is not None:
        in_specs += [pl.BlockSpec(memory_space=pl.ANY)] * 3
        aliases = {len(args) + t: t for t in range(3)}
        args += list(prev)
    return pl.pallas_call(
        _outproj_kernel,
        grid=(n // tm,),
        in_specs=in_specs,
        out_specs=[pl.BlockSpec((tm, d), orow), pl.BlockSpec((tm, d), orow), pl.BlockSpec((tm, LANES), orow)],
        out_shape=[jax.ShapeDtypeStruct((n_total, d), F32), jax.ShapeDtypeStruct((n_total, d), F32),
                   jax.ShapeDtypeStruct((n_total, LANES), F32)],
        input_output_aliases=aliases,
        compiler_params=_params(("arbitrary",)),
        name="outproj",
    )(*args)


def _row_out(x_ref, r, dst_hbm, row, sem):
    return pltpu.make_async_copy(x_ref.at[pl.ds(r, 1), :], dst_hbm.at[pl.ds(row, 1), :], sem)


def _zero_fill_copies(z_ref, xs_hbm, start, length, sem):
    sub = 8
    head = jnp.minimum((-start) & (sub - 1), length)
    body0 = start + head
    body = length - head
    tail0 = body0 + (body & -sub)
    row = lambda off: pltpu.make_async_copy(z_ref.at[pl.ds(0, 1), :], xs_hbm.at[pl.ds(off, 1), :], sem)
    pairs = [(k < head, row(start + k)) for k in range(sub - 1)]
    bit = z_ref.shape[0]
    while bit >= sub:
        off = pl.multiple_of(body0 + (body & (-2 * bit)), sub)
        pairs.append(((body & bit) != 0,
                      pltpu.make_async_copy(z_ref.at[pl.ds(0, bit), :], xs_hbm.at[pl.ds(off, bit), :], sem)))
        bit //= 2
    pairs += [(k < (body & (sub - 1)), row(tail0 + k)) for k in range(sub - 1)]
    return pairs


def _scatter_kernel(ps_ref, pl_ref, pos_ref, x_ref, xs_hbm, z_ref, stage, sem, zsem, *, tm):
    m = pl.program_id(0)
    last = pl.num_programs(0) - 1
    slot = m % 2

    def drain(s):
        def go(r, carry):
            _row_out(stage.at[s], r, xs_hbm, 0, sem.at[s]).wait()
            _row_out(stage.at[s], r, xs_hbm, 0, sem.at[s]).wait()
            return carry
        lax.fori_loop(0, tm, go, 0, unroll=8)

    @pl.when(m == 0)
    def _():
        z_ref[...] = jnp.zeros_like(z_ref)

        def fill(e, carry):
            for pred, cp in _zero_fill_copies(z_ref, xs_hbm, ps_ref[e], pl_ref[e], zsem):
                pl.when(pred)(cp.start)
            return carry
        lax.fori_loop(0, N_EXPERTS, fill, 0)

    @pl.when(m >= 2)
    def _():
        drain(slot)

    stage[slot] = x_ref[...]

    def issue(r, carry):
        _row_out(stage.at[slot], r, xs_hbm, pos_ref[0, 0, 2 * r], sem.at[slot]).start()
        _row_out(stage.at[slot], r, xs_hbm, pos_ref[0, 0, 2 * r + 1], sem.at[slot]).start()
        return carry
    lax.fori_loop(0, tm, issue, 0, unroll=8)

    @pl.when(m == last)
    def _():
        @pl.when(m >= 1)
        def _():
            drain(1 - slot)
        drain(slot)

        def settle(e, carry):
            for pred, cp in _zero_fill_copies(z_ref, xs_hbm, ps_ref[e], pl_ref[e], zsem):
                pl.when(pred)(cp.wait)
            return carry
        lax.fori_loop(0, N_EXPERTS, settle, 0)


def _scatter(pad_start, pad_len, pos, x, tm, rows, tm_rows):
    n, d = x.shape
    return pl.pallas_call(
        functools.partial(_scatter_kernel, tm=tm),
        grid_spec=pltpu.PrefetchScalarGridSpec(
            num_scalar_prefetch=2,
            grid=(n // tm,),
            in_specs=[pl.BlockSpec((1, 1, 2 * tm), lambda m, *_: (m, 0, 0), memory_space=pltpu.SMEM),
                      pl.BlockSpec((tm, d), lambda m, *_: (m, 0))],
            out_specs=pl.BlockSpec(memory_space=pl.ANY),
            scratch_shapes=[pltpu.VMEM((tm_rows // 2, d), x.dtype), pltpu.VMEM((2, tm, d), x.dtype),
                            pltpu.SemaphoreType.DMA((2,)), pltpu.SemaphoreType.DMA(())]),
        out_shape=jax.ShapeDtypeStruct((rows, d), x.dtype),
        compiler_params=_params(("arbitrary",)),
        name="scatter",
    )(pad_start, pad_len, pos, x)


def _weight_copies(e, slot, w_hbm, w_stage, sem):
    return [pltpu.make_async_copy(w_hbm[t].at[e], w_stage[t].at[slot], sem.at[slot, t]) for t in range(3)]


def _experts_kernel(te_ref, nu_ref, nx_ref, ig_ref, x_ref, wg_hbm, wu_hbm, wd_hbm, o_ref,
                    sg, su, sd, g_scr, u_scr, d_scr, sem):
    m = pl.program_id(0)
    w_hbm, w_stage = (wg_hbm, wu_hbm, wd_hbm), (sg, su, sd)

    @pl.when(m < nu_ref[0])
    def _():
        e = te_ref[m]
        new_expert = (m == 0) | (e != te_ref[jnp.maximum(m - 1, 0)])
        slot = ig_ref[m] % 2

        @pl.when(m == 0)
        def _():
            for cp in _weight_copies(e, slot, w_hbm, w_stage, sem):
                cp.start()

        @pl.when(new_expert)
        def _():
            for cp in _weight_copies(e, slot, w_hbm, w_stage, sem):
                cp.wait()

            @pl.when(nx_ref[m] != e)
            def _():
                for cp in _weight_copies(nx_ref[m], 1 - slot, w_hbm, w_stage, sem):
                    cp.start()

            g_scr[...] = sg[slot].astype(BF16)
            u_scr[...] = su[slot].astype(BF16)
            d_scr[...] = sd[slot].astype(BF16)

        x = x_ref[...].astype(BF16)
        g = jnp.dot(x, g_scr[...], preferred_element_type=F32)
        u = jnp.dot(x, u_scr[...], preferred_element_type=F32)
        hid = ((g * _sigmoid(g)) * u).astype(BF16)
        o_ref[...] = jnp.dot(hid, d_scr[...], preferred_element_type=F32)


def _experts(tile_expert, n_used, tile_next, tile_group, xs, wg, wu, wd, tm):
    rows, d = xs.shape
    ff = wg.shape[2]
    tile = lambda m, te, nu, *_: (jnp.minimum(m, jnp.maximum(nu[0] - 1, 0)), 0)
    hbm = pl.BlockSpec(memory_space=pl.ANY)
    return pl.pallas_call(
        _experts_kernel,
        grid_spec=pltpu.PrefetchScalarGridSpec(
            num_scalar_prefetch=4,
            grid=(rows // tm,),
            in_specs=[pl.BlockSpec((tm, d), tile), hbm, hbm, hbm],
            out_specs=pl.BlockSpec((tm, d), tile),
            scratch_shapes=[pltpu.VMEM((2, d, ff), F32), pltpu.VMEM((2, d, ff), F32), pltpu.VMEM((2, ff, d), F32),
                            pltpu.VMEM((d, ff), BF16), pltpu.VMEM((d, ff), BF16), pltpu.VMEM((ff, d), BF16),
                            pltpu.SemaphoreType.DMA((2, 3))]),
        out_shape=jax.ShapeDtypeStruct((rows, d), F32),
        compiler_params=_params(("arbitrary",)),
        name="experts",
    )(tile_expert, n_used, tile_next, tile_group, xs, wg, wu, wd)


def _row_in(src_hbm, row, dst, r, sem):
    return pltpu.make_async_copy(src_hbm.at[pl.ds(row, 1), :], dst.at[pl.ds(r, 1), :], sem)


def _combine_kernel(pos_ref, nxt_ref, x_ref, r_ref, fw_ref, ys_hbm, o_ref, buf, sem, *, tm, ntiles):
    m = pl.program_id(0)
    slot = m % 2

    def issue(tbl, s):
        def go(r, carry):
            _row_in(ys_hbm, tbl[0, 0, 2 * r], buf.at[s, 0], r, sem.at[s]).start()
            _row_in(ys_hbm, tbl[0, 0, 2 * r + 1], buf.at[s, 1], r, sem.at[s]).start()
            return carry
        lax.fori_loop(0, tm, go, 0, unroll=8)

    @pl.when(m == 0)
    def _():
        issue(pos_ref, 0)

    @pl.when(m + 1 < ntiles)
    def _():
        issue(nxt_ref, 1 - slot)

    def drain(r, carry):
        _row_in(ys_hbm, 0, buf.at[slot, 0], r, sem.at[slot]).wait()
        _row_in(ys_hbm, 0, buf.at[slot, 1], r, sem.at[slot]).wait()
        return carry
    lax.fori_loop(0, tm, drain, 0, unroll=8)
    rt = r_ref[...]
    x = x_ref[...] + (buf[slot, 0] * rt[:, 2:3] + buf[slot, 1] * rt[:, 3:4])
    ms = jnp.mean(x * x, axis=-1, keepdims=True)
    o_ref[...] = (x * lax.rsqrt(ms + NORM_EPS)) * fw_ref[...]


def _combine(pos, x2, r, fw, ys, tm, row0, n):
    d = x2.shape[1]
    blk0 = row0 // tm
    ntiles = n // tm
    smem = functools.partial(pl.BlockSpec, memory_space=pltpu.SMEM)
    return pl.pallas_call(
        functools.partial(_combine_kernel, tm=tm, ntiles=ntiles),
        grid=(ntiles,),
        in_specs=[smem((1, 1, 2 * tm), lambda m: (blk0 + m, 0, 0)),
                  smem((1, 1, 2 * tm), lambda m: (blk0 + jnp.minimum(m + 1, ntiles - 1), 0, 0)),
                  pl.BlockSpec((tm, d), lambda m: (blk0 + m, 0)),
                  pl.BlockSpec((tm, LANES), lambda m: (blk0 + m, 0)),
                  pl.BlockSpec((1, d), lambda m: (0, 0)),
                  pl.BlockSpec(memory_space=pl.ANY)],
        out_specs=pl.BlockSpec((tm, d), lambda m: (m, 0)),
        out_shape=jax.ShapeDtypeStruct((n, d), F32),
        scratch_shapes=[pltpu.VMEM((2, 2, tm, d), F32), pltpu.SemaphoreType.DMA((2,))],
        compiler_params=_params(("arbitrary",)),
        name="combine",
    )(pos, pos, x2, r, fw, ys)


def _dispatch_tables(r, tm):
    n = r.shape[0]
    tiles = (2 * n) // tm + N_EXPERTS
    ids = jnp.arange(N_EXPERTS, dtype=jnp.int32)
    e = r[:, 0:2].astype(jnp.int32).reshape(-1)
    onehot = (e[:, None] == ids[None, :]).astype(jnp.int32)
    incl = jnp.cumsum(onehot, axis=0)
    counts = incl[-1]
    etiles = (counts + tm - 1) // tm
    tile_end = jnp.cumsum(etiles)
    tile_start = tile_end - etiles
    n_used = tile_end[-1]
    pos = jnp.sum(onehot * (incl - 1 + (tile_start * tm)[None, :]), axis=1)
    m = jnp.minimum(jnp.arange(tiles, dtype=jnp.int32), n_used - 1)
    te = jnp.sum((tile_end[None, :] <= m[:, None]).astype(jnp.int32), axis=1)
    sel = (te[:, None] == ids[None, :]).astype(jnp.int32)
    pick = lambda tbl: jnp.sum(sel * tbl[None, :], axis=1)
    used = counts > 0
    later = used[None, :] & (ids[None, :] > ids[:, None])
    next_used = jnp.min(jnp.where(later, ids[None, :], N_EXPERTS), axis=1)
    next_used = jnp.where(next_used == N_EXPERTS, ids, next_used)
    group = jnp.cumsum(used.astype(jnp.int32)) - 1
    i32 = lambda a: a.astype(jnp.int32)
    return (i32(pos), i32(te), i32(n_used).reshape(1), i32(pick(next_used)), i32(pick(group)),
            i32(tile_start * tm + counts), i32(etiles * tm - counts))


_TILES = dict(
    token_tile=256,
    expert_row_tile=256,
    inproj_column_blocks=2,
    hgrn_heads=4,
    hgrn_time_block=1024,
    sb_block=256,
    sb_heads=8,
    decode_key_block=512,
    decode_dma_block=2048,
)

def kernel(x_prompt, x_sample, cache_sb_k, cache_sb_v, state_hgrn, w_in, hg_lb_logits, hg_norm_w, w_out, norm1_w,
           norm2_w, router_group_w, router_group_b, router_expert_w, router_expert_b, expert_w_gate, expert_w_up,
           expert_w_down, final_norm_w):
    bp, tp, d = x_prompt.shape
    bs, ts, _ = x_sample.shape
    np_, ns = bp * tp, bs * ts
    n = np_ + ns
    heads = w_in.shape[2] // (7 * HEAD_DIM)
    hw = heads * HEAD_DIM
    cfg = _TILES
    tm_tok, tm_moe = cfg["token_tile"], cfg["expert_row_tile"]

    w_out_b = w_out[0].astype(BF16)
    n1 = norm1_w[0].reshape(1, d)
    n2 = norm2_w[0].reshape(1, d)
    hnw = hg_norm_w[0].reshape(1, HEAD_DIM)
    fw = final_norm_w.reshape(1, d)

    xp = x_prompt.reshape(np_, d)
    xs = x_sample.reshape(ns, d)
    w_in_b = w_in[0].astype(BF16)
    w_cols = w_in_b.shape[1] // cfg["inproj_column_blocks"]
    proj = _inproj(xp, xs, n1, w_in_b, tm_tok, w_cols)

    ohg_p, st_p = _hgrn(proj, 0, hg_lb_logits, hnw, None, bp, tp, heads, 0, cfg["hgrn_heads"],
                        min(tp, cfg["hgrn_time_block"]))
    ohg_s, st_s = _hgrn(proj, np_, hg_lb_logits, hnw, state_hgrn[0], bs, ts, heads, 0, heads, ts)
    osb_p, k_p, v_p = _sb_prompt(proj, bp, tp, heads, 4 * heads, cfg["sb_block"], cfg["sb_heads"])
    osb_s = _sb_decode(proj, np_, cache_sb_k[0], cache_sb_v[0], bs, ts, heads, 4 * heads,
                       cfg["decode_key_block"], cfg["decode_dma_block"])

    rw = jnp.concatenate([router_group_w[0], router_expert_w[0]], axis=1)
    rw = jnp.pad(rw, ((0, 0), (0, LANES - rw.shape[1])))
    rwh = rw.astype(BF16)
    rw2 = jnp.concatenate([rwh, (rw - rwh.astype(F32)).astype(BF16)], axis=1)
    rb = jnp.concatenate([router_group_b[0], router_expert_b[0]])
    rb = jnp.pad(rb, (0, LANES - rb.shape[0])).reshape(1, LANES)

    wa, wb = w_out_b[:hw], w_out_b[hw:]
    outs = _outproj(xp, ohg_p, osb_p, wa, wb, n2, rw2, rb, 2 * tm_tok, n, 0, None)
    x2, xn2, r = _outproj(xs, ohg_s, osb_s, wa, wb, n2, rw2, rb, tm_tok, n, np_, outs)

    pos, tile_expert, n_used, tile_next, tile_group, pad_start, pad_len = _dispatch_tables(r, tm_moe)
    pos = pos.reshape(n // tm_tok, 1, 2 * tm_tok)
    xg = _scatter(pad_start, pad_len, pos, xn2, tm_tok, tile_expert.shape[0] * tm_moe, tm_moe)
    ys = _experts(tile_expert, n_used, tile_next, tile_group, xg,
                  expert_w_gate[0], expert_w_up[0], expert_w_down[0], tm_moe)
    y_p = _combine(pos, x2, r, fw, ys, tm_tok, 0, np_)
    y_s = _combine(pos, x2, r, fw, ys, tm_tok, np_, ns)

    heads_s = lambda a: a.reshape(bs, ts, heads, HEAD_DIM).transpose(0, 2, 1, 3)[None]
    k_s = heads_s(proj[np_:, 5 * hw:6 * hw])
    v_s = heads_s(proj[np_:, 6 * hw:7 * hw])
    return (y_p.reshape(bp, tp, d), y_s.reshape(bs, ts, d), k_p[None], v_p[None], st_p[None],
            k_s, v_s, st_s[None])
```
